```python
import jax, jax.numpy as jnp
from jax import lax
import numpy as np

D_MODEL = 1024
BATCH = 2
SEQ = 16384
DEPTH = 4

N_MEM = 256
EPS = 1e-6
D_RNN = D_MODEL
LRU_BLOCKS = 8
LRU_BLOCK = D_RNN // LRU_BLOCKS
LRU_C = 8.0
CONV_W = 4
FOX_HEADS = 8
FOX_HEAD_DIM = D_MODEL // FOX_HEADS
D_FOX = FOX_HEADS * FOX_HEAD_DIM
Q_BLOCK = 128
MEM_HEADS = 4
MEM_HEAD_DIM = D_MODEL // MEM_HEADS
D_MEMATT = MEM_HEADS * MEM_HEAD_DIM
N_BRANCH = 3
D_FF = 4 * D_MODEL
OFF_LRU = 0
OFF_Q = OFF_LRU + D_RNN
OFF_K = OFF_Q + D_FOX
OFF_V = OFF_K + D_FOX
OFF_MQ = OFF_V + D_FOX
OFF_GATE = OFF_MQ + D_MEMATT
OFF_F = OFF_GATE + N_BRANCH * D_MODEL
N_IN = OFF_F + FOX_HEADS

kernel_name = 'hybrid_rglru_fox_memory_block'


def rms_norm(x, g):
    xf = x.astype(jnp.float32)
    y = xf * lax.rsqrt(jnp.mean(xf * xf, axis=-1, keepdims=True) + EPS)
    return (y * g.astype(jnp.float32)).astype(x.dtype)


def causal_depthwise_conv(x, w, b):
    y = lax.conv_general_dilated(x, w[:, None, :].astype(x.dtype), window_strides=(1,),
                                 padding=[(CONV_W - 1, 0)], dimension_numbers=('NWC', 'WIO', 'NWC'),
                                 feature_group_count=x.shape[-1])
    return y + b.astype(x.dtype)


def _linear_recurrence_combine(left, right):
    a_l, b_l = left
    a_r, b_r = right
    return a_l * a_r, a_r * b_l + b_r


def rg_lru(x, w_r, b_r, w_i, b_i, lam):
    bsz, s, c = x.shape
    xb = x.reshape(bsz, s, LRU_BLOCKS, LRU_BLOCK)
    r = jax.nn.sigmoid((jnp.einsum('bshi,hij->bshj', xb, w_r) + b_r).astype(jnp.float32)).reshape(bsz, s, c)
    i = jax.nn.sigmoid((jnp.einsum('bshi,hij->bshj', xb, w_i) + b_i).astype(jnp.float32)).reshape(bsz, s, c)
    log_a = -LRU_C * r * jax.nn.softplus(-lam.astype(jnp.float32))
    a = jnp.exp(log_a)
    gated_x = jnp.sqrt(-jnp.expm1(2.0 * log_a)) * i * x.astype(jnp.float32)
    _, h = lax.associative_scan(_linear_recurrence_combine, (a, gated_x), axis=1)
    return h.astype(x.dtype)


def forgetting_attention(q, k, v, log_f):
    bsz, s, nh, dh = q.shape
    nb = s // Q_BLOCK
    c = jnp.cumsum(log_f, axis=1).transpose(0, 2, 1)
    q_blocks = q.reshape(bsz, nb, Q_BLOCK, nh, dh).transpose(1, 0, 2, 3, 4)
    c_blocks = c.reshape(bsz, nh, nb, Q_BLOCK).transpose(2, 0, 1, 3)
    k_pos = jnp.arange(s)
    scale = dh ** -0.5

    def one_block(args):
        q_blk, c_blk, blk_idx = args
        logits = jnp.einsum('bqhd,bkhd->bhqk', q_blk, k, preferred_element_type=jnp.float32) * scale
        logits = logits + c_blk[..., :, None] - c[:, :, None, :]
        q_pos = blk_idx * Q_BLOCK + jnp.arange(Q_BLOCK)
        logits = jnp.where(k_pos[None, :] <= q_pos[:, None], logits, -jnp.inf)
        p = jax.nn.softmax(logits, axis=-1)
        return jnp.einsum('bhqk,bkhd->bqhd', p.astype(v.dtype), v)

    out = lax.map(one_block, (q_blocks, c_blocks, jnp.arange(nb)))
    return out.transpose(1, 0, 2, 3, 4).reshape(bsz, s, nh, dh)


def memory_attention(q, mem_k, mem_v):
    scale = q.shape[-1] ** -0.5
    logits = jnp.einsum('bshd,bmhd->bhsm', q, mem_k, preferred_element_type=jnp.float32) * scale
    p = jax.nn.softmax(logits, axis=-1)
    return jnp.einsum('bhsm,bmhd->bshd', p.astype(mem_v.dtype), mem_v)


def mixer_sublayer(h, mem, g_pre, w_in, conv_w, conv_b, w_r, b_r, w_i, b_i, lam, b_f,
                   g_mem, w_mem_kv, b_gate, w_pa, w_pb, w_pc, w_o, g_post):
    bsz, s, _ = h.shape
    n_mem = mem.shape[1]
    u = rms_norm(h, g_pre)
    z = u @ w_in
    y_a = rg_lru(causal_depthwise_conv(z[..., OFF_LRU:OFF_LRU + D_RNN], conv_w, conv_b),
                 w_r, b_r, w_i, b_i, lam)
    q = z[..., OFF_Q:OFF_Q + D_FOX].reshape(bsz, s, FOX_HEADS, FOX_HEAD_DIM)
    k = z[..., OFF_K:OFF_K + D_FOX].reshape(bsz, s, FOX_HEADS, FOX_HEAD_DIM)
    v = z[..., OFF_V:OFF_V + D_FOX].reshape(bsz, s, FOX_HEADS, FOX_HEAD_DIM)
    log_f = jax.nn.log_sigmoid(z[..., OFF_F:OFF_F + FOX_HEADS].astype(jnp.float32) + b_f.astype(jnp.float32))
    y_b = forgetting_attention(q, k, v, log_f).reshape(bsz, s, D_FOX)
    kv = rms_norm(mem, g_mem) @ w_mem_kv
    mem_k = kv[..., :D_MEMATT].reshape(bsz, n_mem, MEM_HEADS, MEM_HEAD_DIM)
    mem_v = kv[..., D_MEMATT:].reshape(bsz, n_mem, MEM_HEADS, MEM_HEAD_DIM)
    mq = z[..., OFF_MQ:OFF_MQ + D_MEMATT].reshape(bsz, s, MEM_HEADS, MEM_HEAD_DIM)
    y_c = memory_attention(mq, mem_k, mem_v).reshape(bsz, s, D_MEMATT)
    gates = jax.nn.sigmoid(z[..., OFF_GATE:OFF_GATE + N_BRANCH * D_MODEL].reshape(bsz, s, N_BRANCH, D_MODEL) + b_gate)
    merged = (gates[:, :, 0] * (y_a @ w_pa) + gates[:, :, 1] * (y_b @ w_pb)
              + gates[:, :, 2] * (y_c @ w_pc))
    return h + rms_norm(merged @ w_o, g_post)


def mlp_sublayer(h, g_pre, w_up, w_down, g_post):
    u = rms_norm(h, g_pre)
    act = jnp.square(jax.nn.relu(u @ w_up))
    return h + rms_norm(act @ w_down, g_post)


def setup_inputs(seed: int = 0) -> dict:
    key = jax.random.key(seed)
    ks = jax.random.split(key, 26)
    f32 = jnp.float32

    def nrm(k, shape, fan_in):
        return jax.random.normal(k, shape, f32) * (fan_in ** -0.5)

    def gain(k):
        return 1.0 + 0.02 * jax.random.normal(k, (DEPTH, D_MODEL), f32)

    def small(k, shape):
        return 0.01 * jax.random.normal(k, shape, f32)

    a_c = jax.random.uniform(ks[8], (DEPTH, D_RNN), f32, minval=0.9, maxval=0.999)
    a0 = a_c ** (1.0 / LRU_C)
    lru_lambda = jnp.log(a0) - jnp.log1p(-a0)
    return {
        'x': jax.random.normal(ks[0], (BATCH, SEQ, D_MODEL), f32),
        'mem': jax.random.normal(ks[1], (BATCH, N_MEM, D_MODEL), f32),
        'g_mix_pre': gain(ks[2]),
        'w_in': nrm(ks[3], (DEPTH, D_MODEL, N_IN), D_MODEL),
        'conv_w': nrm(ks[4], (DEPTH, CONV_W, D_RNN), CONV_W),
        'conv_b': small(ks[5], (DEPTH, D_RNN)),
        'w_r': nrm(ks[6], (DEPTH, LRU_BLOCKS, LRU_BLOCK, LRU_BLOCK), LRU_BLOCK),
        'b_r': small(ks[7], (DEPTH, LRU_BLOCKS, LRU_BLOCK)),
        'w_i': nrm(ks[9], (DEPTH, LRU_BLOCKS, LRU_BLOCK, LRU_BLOCK), LRU_BLOCK),
        'b_i': small(ks[10], (DEPTH, LRU_BLOCKS, LRU_BLOCK)),
        'lru_lambda': lru_lambda,
        'b_f': jax.random.uniform(ks[11], (DEPTH, FOX_HEADS), f32, minval=1.0, maxval=6.0),
        'g_mem': gain(ks[12]),
        'w_mem_kv': nrm(ks[13], (DEPTH, D_MODEL, 2 * D_MEMATT), D_MODEL),
        'b_gate': small(ks[14], (DEPTH, N_BRANCH, D_MODEL)),
        'w_pa': nrm(ks[15], (DEPTH, D_RNN, D_MODEL), D_RNN),
        'w_pb': nrm(ks[16], (DEPTH, D_FOX, D_MODEL), D_FOX),
        'w_pc': nrm(ks[17], (DEPTH, D_MEMATT, D_MODEL), D_MEMATT),
        'w_o': nrm(ks[18], (DEPTH, D_MODEL, D_MODEL), D_MODEL),
        'g_mix_post': gain(ks[19]),
        'g_mlp_pre': gain(ks[20]),
        'w_up': nrm(ks[21], (DEPTH, D_MODEL, D_FF), D_MODEL),
        'w_down': nrm(ks[22], (DEPTH, D_FF, D_MODEL), D_FF),
        'g_mlp_post': gain(ks[23]),
    }


def reference(x, mem, g_mix_pre, w_in, conv_w, conv_b, w_r, b_r, w_i, b_i, lru_lambda, b_f,
              g_mem, w_mem_kv, b_gate, w_pa, w_pb, w_pc, w_o, g_mix_post,
              g_mlp_pre, w_up, w_down, g_mlp_post):
    h = x
    for l in range(DEPTH):
        h = mixer_sublayer(h, mem, g_mix_pre[l], w_in[l], conv_w[l], conv_b[l], w_r[l], b_r[l],
                           w_i[l], b_i[l], lru_lambda[l], b_f[l], g_mem[l], w_mem_kv[l], b_gate[l],
                           w_pa[l], w_pb[l], w_pc[l], w_o[l], g_mix_post[l])
        h = mlp_sublayer(h, g_mlp_pre[l], w_up[l], w_down[l], g_mlp_post[l])
    return h
```

```python
import functools

import jax
import jax.numpy as jnp
from jax import lax
from jax.experimental import pallas as pl
from jax.experimental.pallas import tpu as pltpu

EPS = 1e-6
LRU_BLOCKS = 8
LRU_C = 8.0
CONV_W = 4
FOX_HEADS = 8
MEM_HEADS = 4
N_BRANCH = 3
LOG2E = 1.4426950408889634
NEG_BIG = -1e30

V7X_LANES = 128
V7X_SUBLANES = 8
V7X_VMEM_LIMIT_BYTES = 56 * 1024 * 1024

ROW_TILE = 256
Q_TILE = 512
BIAS_LANES = V7X_LANES
N_SPLIT = 3

F32 = jnp.float32
BF16 = jnp.bfloat16


def _const_spec(shape):
    nd = len(shape)
    return pl.BlockSpec(shape, lambda *_: (0,) * nd, pipeline_mode=pl.Buffered(1))


def _rms(x, g):
    return x * lax.rsqrt(jnp.mean(x * x, axis=-1, keepdims=True) + EPS) * g


def _sigmoid(x):
    return 1.0 / (1.0 + jnp.exp(-x))


def _softplus(x):
    return jnp.maximum(x, 0.0) + jnp.log1p(jnp.exp(-jnp.abs(x)))


def _dot(a, b):
    return jnp.dot(a, b, preferred_element_type=F32)


def _dot_nt(a, b):
    return lax.dot_general(a, b, (((1,), (1,)), ((), ())), preferred_element_type=F32)


def _mem_kv_kernel(mem_ref, g_ref, w_ref, kv_ref):
    u = _rms(mem_ref[0], g_ref[0]).astype(BF16)
    kv_ref[0, 0] = _dot(u, w_ref[0]).astype(BF16)


def _mem_kv(mem, g_mem, w_mem_kv):
    depth, d_model, n_kv = w_mem_kv.shape
    bsz, n_mem, _ = mem.shape
    return pl.pallas_call(
        _mem_kv_kernel,
        grid=(depth, bsz),
        in_specs=[
            pl.BlockSpec((1, n_mem, d_model), lambda l, b: (b, 0, 0)),
            pl.BlockSpec((1, 1, d_model), lambda l, b: (l, 0, 0)),
            pl.BlockSpec((1, d_model, n_kv), lambda l, b: (l, 0, 0)),
        ],
        out_specs=pl.BlockSpec((1, 1, n_mem, n_kv), lambda l, b: (l, b, 0, 0)),
        out_shape=jax.ShapeDtypeStruct((depth, bsz, n_mem, n_kv), BF16),
        compiler_params=pltpu.CompilerParams(
            dimension_semantics=("arbitrary", "arbitrary"), vmem_limit_bytes=V7X_VMEM_LIMIT_BYTES),
        name="mem_kv",
    )(mem, g_mem.reshape(depth, 1, d_model), w_mem_kv.astype(BF16))


def _mixer_in_kernel(h_ref, gpre_ref, wlru_ref, wk_ref, wqvt_ref, wmq_ref, wgate_ref, wf_ref,
                     convw_ref, convb_ref, wri_ref, br_ref, bi_ref, lam_ref, bf_ref, kv_ref,
                     bgate_ref, wpa_ref, wpc_ref,
                     qt_ref, k_ref, vt_ref, part_ref, g1_ref,
                     zbuf, a_scr, b_scr, hcar, ccar, *, q_scale):
    tm, d = h_ref.shape[1], h_ref.shape[2]
    n_grp = tm // V7X_SUBLANES
    lru_blk = d // LRU_BLOCKS
    fox_dh = d // FOX_HEADS
    mem_dh = d // MEM_HEADS

    @pl.when(pl.program_id(1) == 0)
    def _():
        zbuf[0:V7X_SUBLANES, :] = jnp.zeros((V7X_SUBLANES, d), F32)
        hcar[...] = jnp.zeros_like(hcar)
        ccar[...] = jnp.zeros_like(ccar)

    u = _rms(h_ref[0], gpre_ref[...]).astype(BF16)

    zbuf[V7X_SUBLANES:V7X_SUBLANES + tm, :] = _dot(u, wlru_ref[...])
    xc = convb_ref[...]
    for kk in range(CONV_W):
        off = V7X_SUBLANES - (CONV_W - 1) + kk
        xc = xc + convw_ref[kk:kk + 1, :] * zbuf[off:off + tm, :]
    zbuf[0:V7X_SUBLANES, :] = zbuf[tm:tm + V7X_SUBLANES, :]
    xcb = xc.astype(BF16)
    neg_c_sp = -LRU_C * _softplus(-lam_ref[...])
    for hb in range(LRU_BLOCKS):
        sl = slice(hb * lru_blk, (hb + 1) * lru_blk)
        ri = _dot(xcb[:, sl], wri_ref[hb])
        r = _sigmoid(ri[:, :lru_blk] + br_ref[:, sl])
        i = _sigmoid(ri[:, lru_blk:] + bi_ref[:, sl])
        a = jnp.exp(r * neg_c_sp[:, sl])
        a_scr[:, sl] = a
        b_scr[:, sl] = jnp.sqrt(1.0 - a * a) * i * xc[:, sl]
    a3 = a_scr[...].reshape(n_grp, V7X_SUBLANES, d)
    b3 = b_scr[...].reshape(n_grp, V7X_SUBLANES, d)
    sub = lax.broadcasted_iota(jnp.int32, a3.shape, 1)
    step = 1
    while step < V7X_SUBLANES:
        keep = sub >= step
        a_prev = jnp.where(keep, pltpu.roll(a3, step, 1), 1.0)
        b_prev = jnp.where(keep, pltpu.roll(b3, step, 1), 0.0)
        b3 = a3 * b_prev + b3
        a3 = a3 * a_prev
        step *= 2
    carry = hcar[...]
    for g in range(n_grp):
        hg = a3[g] * carry + b3[g]
        b_scr[g * V7X_SUBLANES:(g + 1) * V7X_SUBLANES, :] = hg
        carry = hg[V7X_SUBLANES - 1:V7X_SUBLANES, :]
    hcar[...] = carry
    pa = _dot(b_scr[...].astype(BF16), wpa_ref[...])

    kz = _dot(u, wk_ref[...]).astype(BF16)
    zt = _dot_nt(wqvt_ref[...], u)
    qt = (zt[:d] * q_scale).astype(BF16)
    vt = zt[d:].astype(BF16)
    zf = _dot(u, wf_ref[...]) + bf_ref[...]
    lf = (jnp.minimum(zf, 0.0) - jnp.log1p(jnp.exp(-jnp.abs(zf)))) * LOG2E
    c3 = lf.reshape(n_grp, V7X_SUBLANES, BIAS_LANES)
    sub = lax.broadcasted_iota(jnp.int32, c3.shape, 1)
    step = 1
    while step < V7X_SUBLANES:
        c3 = c3 + jnp.where(sub >= step, pltpu.roll(c3, step, 1), 0.0)
        step *= 2
    ccarry = ccar[...]
    c_rows = []
    for g in range(n_grp):
        cg = c3[g] + ccarry
        c_rows.append(cg)
        ccarry = cg[V7X_SUBLANES - 1:V7X_SUBLANES, :]
    ccar[...] = ccarry
    c = jnp.concatenate(c_rows, axis=0)
    c_hi = c.astype(BF16).astype(F32)
    c_mid = (c - c_hi).astype(BF16).astype(F32)
    c_lo = (c - c_hi - c_mid).astype(BF16).astype(F32)
    lane = lax.broadcasted_iota(jnp.int32, c.shape, 1)
    piece = lane % N_SPLIT
    c_split = jnp.where(piece == 0, c_hi, jnp.where(piece == 1, c_mid, c_lo))
    n_bias = FOX_HEADS * N_SPLIT
    k_bias = jnp.where(lane < n_bias, -c_split, jnp.where(lane < 2 * n_bias, 1.0, 0.0)).astype(BF16)
    c_split_t = pltpu.roll(c_split.T, n_bias, 0)
    row = lax.broadcasted_iota(jnp.int32, c_split_t.shape, 0)
    for hh in range(FOX_HEADS):
        sl = slice(hh * fox_dh, (hh + 1) * fox_dh)
        lo = hh * N_SPLIT
        ones_rows = (row >= lo) & (row < lo + N_SPLIT)
        c_rows_sel = (row >= n_bias + lo) & (row < n_bias + lo + N_SPLIT)
        q_bias = jnp.where(ones_rows, 1.0, jnp.where(c_rows_sel, c_split_t, 0.0)).astype(BF16)
        k_ref[0, hh, 0, :, 0:fox_dh] = kz[:, sl]
        k_ref[0, hh, 0, :, fox_dh:fox_dh + BIAS_LANES] = k_bias
        qt_ref[0, hh, 0:fox_dh, :] = qt[sl, :]
        qt_ref[0, hh, fox_dh:fox_dh + BIAS_LANES, :] = q_bias
        vt_ref[0, hh, 0] = vt[sl, :]

    mq = (_dot(u, wmq_ref[...]) * (mem_dh ** -0.5)).astype(BF16)
    yc = []
    for hh in range(MEM_HEADS):
        sl = slice(hh * mem_dh, (hh + 1) * mem_dh)
        lg = _dot_nt(mq[:, sl], kv_ref[0, :, sl])
        e = jnp.exp(lg - jnp.max(lg, axis=-1, keepdims=True))
        num = _dot(e.astype(BF16), kv_ref[0, :, d + hh * mem_dh:d + (hh + 1) * mem_dh])
        yc.append((num / jnp.sum(e, axis=-1, keepdims=True)).astype(BF16))
    pc = _dot(jnp.concatenate(yc, axis=1), wpc_ref[...])

    g0 = _sigmoid(_dot(u, wgate_ref[:, 0:d]) + bgate_ref[0:1, :])
    g2 = _sigmoid(_dot(u, wgate_ref[:, 2 * d:3 * d]) + bgate_ref[2:3, :])
    part_ref[0] = g0 * pa + g2 * pc
    g1_ref[0] = _sigmoid(_dot(u, wgate_ref[:, d:2 * d]) + bgate_ref[1:2, :])


def _mixer_in(h, kv, p):
    bsz, seq, d = h.shape
    tm = ROW_TILE
    n_t = seq // tm
    fox_dh = d // FOX_HEADS
    n_mem, n_kv = kv.shape[1], kv.shape[2]
    qk_dim = fox_dh + BIAS_LANES
    row_spec = pl.BlockSpec((1, tm, d), lambda b, j: (b, j, 0))
    in_specs = [
        row_spec,
        _const_spec((1, d)),
        _const_spec((d, d)), _const_spec((d, d)), _const_spec((2 * d, d)), _const_spec((d, d)),
        _const_spec((d, N_BRANCH * d)), _const_spec((d, BIAS_LANES)),
        _const_spec((CONV_W, d)), _const_spec((1, d)),
        _const_spec((LRU_BLOCKS, d // LRU_BLOCKS, 2 * d // LRU_BLOCKS)),
        _const_spec((1, d)), _const_spec((1, d)), _const_spec((1, d)), _const_spec((1, BIAS_LANES)),
        pl.BlockSpec((1, n_mem, n_kv), lambda b, j: (b, 0, 0)),
        _const_spec((N_BRANCH, d)), _const_spec((d, d)), _const_spec((d, d)),
    ]
    out_specs = [
        pl.BlockSpec((1, FOX_HEADS, qk_dim, tm), lambda b, j: (b, 0, 0, j)),
        pl.BlockSpec((1, FOX_HEADS, 1, tm, qk_dim), lambda b, j: (b, 0, j, 0, 0)),
        pl.BlockSpec((1, FOX_HEADS, 1, fox_dh, tm), lambda b, j: (b, 0, j, 0, 0)),
        row_spec, row_spec,
    ]
    out_shape = [
        jax.ShapeDtypeStruct((bsz, FOX_HEADS, qk_dim, seq), BF16),
        jax.ShapeDtypeStruct((bsz, FOX_HEADS, n_t, tm, qk_dim), BF16),
        jax.ShapeDtypeStruct((bsz, FOX_HEADS, n_t, fox_dh, tm), BF16),
        jax.ShapeDtypeStruct((bsz, seq, d), F32),
        jax.ShapeDtypeStruct((bsz, seq, d), F32),
    ]
    scratch = [
        pltpu.VMEM((tm + V7X_SUBLANES, d), F32),
        pltpu.VMEM((tm, d), F32), pltpu.VMEM((tm, d), F32),
        pltpu.VMEM((1, d), F32), pltpu.VMEM((1, BIAS_LANES), F32),
    ]
    return pl.pallas_call(
        functools.partial(_mixer_in_kernel, q_scale=(fox_dh ** -0.5) * LOG2E),
        grid=(bsz, n_t),
        in_specs=in_specs, out_specs=out_specs, out_shape=out_shape, scratch_shapes=scratch,
        compiler_params=pltpu.CompilerParams(
            dimension_semantics=("arbitrary", "arbitrary"), vmem_limit_bytes=V7X_VMEM_LIMIT_BYTES),
        name="mixer_in",
    )(h, p["g_pre"], p["w_lru"], p["w_k"], p["w_qvt"], p["w_mq"], p["w_gate"], p["w_f"],
      p["conv_w"], p["conv_b"], p["w_ri"], p["b_r"], p["b_i"], p["lam"], p["b_f"], kv,
      p["b_gate"], p["w_pa"], p["w_pc"])


def _fox_kernel(qt_ref, k_ref, vt_ref, o_ref, m_scr, l_scr, acc_scr):
    tq = qt_ref.shape[3]
    tc = k_ref.shape[3]
    per_q = tq // tc
    i = pl.program_id(2)
    m_scr[...] = jnp.full_like(m_scr, NEG_BIG)
    l_scr[...] = jnp.zeros_like(l_scr)
    acc_scr[...] = jnp.zeros_like(acc_scr)
    qt = qt_ref[0, 0]

    def chunk(c, diag):
        s = _dot(k_ref[0, 0, c], qt)
        if diag is not None:
            key = lax.broadcasted_iota(jnp.int32, s.shape, 0) + diag * tc
            qry = lax.broadcasted_iota(jnp.int32, s.shape, 1)
            s = jnp.where(key <= qry, s, NEG_BIG)
        m_old = m_scr[...]
        m_new = jnp.maximum(m_old, jnp.max(s, axis=0, keepdims=True))
        p = jnp.exp2(s - m_new)
        alpha = jnp.exp2(m_old - m_new)
        l_scr[...] = alpha * l_scr[...] + jnp.sum(p, axis=0, keepdims=True)
        acc_scr[...] = alpha * acc_scr[...] + _dot(vt_ref[0, 0, c], p.astype(BF16))
        m_scr[...] = m_new

    for dgl in range(per_q):
        chunk(i * per_q + dgl, dgl)

    def body(jj, carry):
        for sub in range(per_q):
            chunk(jj * per_q + sub, None)
        return carry

    lax.fori_loop(0, i, body, 0)
    o_ref[0] = (acc_scr[...] / l_scr[...]).T.astype(o_ref.dtype)


def _fox_attention(qt, k, vt, d_model):
    bsz, n_heads, qk_dim, seq = qt.shape
    n_c, tc = k.shape[2], k.shape[3]
    dh = vt.shape[3]
    tq = Q_TILE
    return pl.pallas_call(
        _fox_kernel,
        grid=(bsz, n_heads, seq // tq),
        in_specs=[
            pl.BlockSpec((1, 1, qk_dim, tq), lambda b, h, i: (b, h, 0, i)),
            pl.BlockSpec((1, 1, n_c, tc, qk_dim), lambda b, h, i: (b, h, 0, 0, 0)),
            pl.BlockSpec((1, 1, n_c, dh, tc), lambda b, h, i: (b, h, 0, 0, 0)),
        ],
        out_specs=pl.BlockSpec((1, tq, dh), lambda b, h, i: (b, i, h)),
        out_shape=jax.ShapeDtypeStruct((bsz, seq, d_model), BF16),
        scratch_shapes=[pltpu.VMEM((1, tq), F32), pltpu.VMEM((1, tq), F32), pltpu.VMEM((dh, tq), F32)],
        compiler_params=pltpu.CompilerParams(
            dimension_semantics=("arbitrary", "arbitrary", "arbitrary"),
            vmem_limit_bytes=V7X_VMEM_LIMIT_BYTES),
        name="fox_attention",
    )(qt, k, vt)


def _post_kernel(yb_ref, part_ref, g1_ref, h_ref, wpb_ref, wo_ref, gpost_ref, gmlp_ref, wup_ref,
                 wdown_ref, gmlp_post_ref, out_ref):
    merged = part_ref[0] + g1_ref[0] * _dot(yb_ref[0], wpb_ref[...])
    h1 = h_ref[0] + _rms(_dot(merged.astype(BF16), wo_ref[...]), gpost_ref[...])
    u = _rms(h1, gmlp_ref[...]).astype(BF16)
    act = jnp.square(jnp.maximum(_dot(u, wup_ref[...]), 0.0))
    out_ref[0] = h1 + _rms(_dot(act.astype(BF16), wdown_ref[...]), gmlp_post_ref[...])


def _post(yb, part, g1, h, p):
    bsz, seq, d = h.shape
    d_ff = p["w_up"].shape[1]
    tm = ROW_TILE
    row_spec = pl.BlockSpec((1, tm, d), lambda b, j: (b, j, 0))
    return pl.pallas_call(
        _post_kernel,
        grid=(bsz, seq // tm),
        in_specs=[row_spec, row_spec, row_spec, row_spec,
                  _const_spec((d, d)), _const_spec((d, d)), _const_spec((1, d)), _const_spec((1, d)),
                  _const_spec((d, d_ff)), _const_spec((d_ff, d)), _const_spec((1, d))],
        out_specs=row_spec,
        out_shape=jax.ShapeDtypeStruct((bsz, seq, d), F32),
        compiler_params=pltpu.CompilerParams(
            dimension_semantics=("arbitrary", "arbitrary"), vmem_limit_bytes=V7X_VMEM_LIMIT_BYTES),
        name="post",
    )(yb, part, g1, h, p["w_pb"], p["w_o"], p["g_post"], p["g_mlp_pre"], p["w_up"], p["w_down"],
      p["g_mlp_post"])


def _layer_params(l, d, g_mix_pre, w_in, conv_w, conv_b, w_r, b_r, w_i, b_i, lru_lambda, b_f,
                  b_gate, w_pa, w_pb, w_pc, w_o, g_mix_post, g_mlp_pre, w_up, w_down, g_mlp_post):
    wi = w_in[l]
    off_q, off_k, off_v, off_mq, off_gate = d, 2 * d, 3 * d, 4 * d, 5 * d
    off_f = off_gate + N_BRANCH * d
    w_f = jnp.repeat(wi[:, off_f:off_f + FOX_HEADS], N_SPLIT, axis=1)
    pad = BIAS_LANES - FOX_HEADS * N_SPLIT
    row = lambda v: v.reshape(1, -1).astype(F32)
    return dict(
        g_pre=row(g_mix_pre[l]),
        w_lru=wi[:, 0:d].astype(BF16),
        w_k=wi[:, off_k:off_k + d].astype(BF16),
        w_qvt=jnp.concatenate([wi[:, off_q:off_q + d].T, wi[:, off_v:off_v + d].T], axis=0).astype(BF16),
        w_mq=wi[:, off_mq:off_mq + d].astype(BF16),
        w_gate=wi[:, off_gate:off_gate + N_BRANCH * d].astype(BF16),
        w_f=jnp.pad(w_f, ((0, 0), (0, pad))).astype(BF16),
        conv_w=conv_w[l].astype(F32), conv_b=row(conv_b[l]),
        w_ri=jnp.concatenate([w_r[l], w_i[l]], axis=-1).astype(BF16),
        b_r=row(b_r[l]), b_i=row(b_i[l]), lam=row(lru_lambda[l]),
        b_f=jnp.pad(jnp.repeat(b_f[l], N_SPLIT), (0, pad)).reshape(1, -1).astype(F32),
        b_gate=b_gate[l].astype(F32),
        w_pa=w_pa[l].astype(BF16), w_pb=w_pb[l].astype(BF16), w_pc=w_pc[l].astype(BF16),
        w_o=w_o[l].astype(BF16), g_post=row(g_mix_post[l]), g_mlp_pre=row(g_mlp_pre[l]),
        w_up=w_up[l].astype(BF16), w_down=w_down[l].astype(BF16), g_mlp_post=row(g_mlp_post[l]),
    )


def kernel(x, mem, g_mix_pre, w_in, conv_w, conv_b, w_r, b_r, w_i, b_i, lru_lambda, b_f, g_mem, w_mem_kv, b_gate, w_pa, w_pb, w_pc, w_o, g_mix_post, g_mlp_pre, w_up, w_down, g_mlp_post):
    bsz, seq, d = x.shape
    depth = w_in.shape[0]
    assert seq % Q_TILE == 0 and Q_TILE % ROW_TILE == 0 and d % (FOX_HEADS * V7X_LANES) == 0
    kv_all = _mem_kv(mem, g_mem, w_mem_kv)
    h = x
    for l in range(depth):
        p = _layer_params(l, d, g_mix_pre, w_in, conv_w, conv_b, w_r, b_r, w_i, b_i, lru_lambda,
                          b_f, b_gate, w_pa, w_pb, w_pc, w_o, g_mix_post, g_mlp_pre, w_up, w_down,
                          g_mlp_post)
        qt, k, vt, part, g1 = _mixer_in(h, kv_all[l], p)
        yb = _fox_attention(qt, k, vt, d)
        h = _post(yb, part, g1, h, p)
    return h
```

```python
import functools

import jax
import jax.numpy as jnp
from jax import lax
from jax.experimental import pallas as pl
from jax.experimental.pallas import tpu as pltpu

EPS = 1e-6
LRU_BLOCKS = 8
LRU_C = 8.0
CONV_W = 4
FOX_HEADS = 8
MEM_HEADS = 4
N_BRANCH = 3
LOG2E = 1.4426950408889634
NEG_BIG = -1e30

V7X_LANES = 128
V7X_SUBLANES = 8
V7X_VMEM_LIMIT_BYTES = 56 * 1024 * 1024

ROW_TILE = 256
Q_TILE = 512
BIAS_LANES = V7X_LANES
N_SPLIT = 3

F32 = jnp.float32
BF16 = jnp.bfloat16


def _const_spec(shape):
    nd = len(shape)
    return pl.BlockSpec(shape, lambda *_: (0,) * nd, pipeline_mode=pl.Buffered(1))


def _rms(x, g):
    return x * lax.rsqrt(jnp.mean(x * x, axis=-1, keepdims=True) + EPS) * g


def _sigmoid(x):
    return 1.0 / (1.0 + jnp.exp(-x))


def _softplus(x):
    return jnp.maximum(x, 0.0) + jnp.log1p(jnp.exp(-jnp.abs(x)))


def _dot(a, b):
    return jnp.dot(a, b, preferred_element_type=F32)


def _dot_nt(a, b):
    return lax.dot_general(a, b, (((1,), (1,)), ((), ())), preferred_element_type=F32)


def _mem_kv_kernel(mem_ref, g_ref, w_ref, kv_ref):
    u = _rms(mem_ref[0], g_ref[0]).astype(BF16)
    kv_ref[0, 0] = _dot(u, w_ref[0]).astype(BF16)


def _mem_kv(mem, g_mem, w_mem_kv):
    depth, d_model, n_kv = w_mem_kv.shape
    bsz, n_mem, _ = mem.shape
    return pl.pallas_call(
        _mem_kv_kernel,
        grid=(depth, bsz),
        in_specs=[
            pl.BlockSpec((1, n_mem, d_model), lambda l, b: (b, 0, 0)),
            pl.BlockSpec((1, 1, d_model), lambda l, b: (l, 0, 0)),
            pl.BlockSpec((1, d_model, n_kv), lambda l, b: (l, 0, 0)),
        ],
        out_specs=pl.BlockSpec((1, 1, n_mem, n_kv), lambda l, b: (l, b, 0, 0)),
        out_shape=jax.ShapeDtypeStruct((depth, bsz, n_mem, n_kv), BF16),
        compiler_params=pltpu.CompilerParams(
            dimension_semantics=("arbitrary", "arbitrary"), vmem_limit_bytes=V7X_VMEM_LIMIT_BYTES),
        name="mem_kv",
    )(mem, g_mem.reshape(depth, 1, d_model), w_mem_kv.astype(BF16))


def _mixer_in_kernel(h_ref, gpre_ref, wlru_ref, wk_ref, wqvt_ref, wmq_ref, wgate_ref, wf_ref,
                     convw_ref, convb_ref, wri_ref, br_ref, bi_ref, lam_ref, bf_ref, kv_ref,
                     bgate_ref, wpa_ref, wpc_ref,
                     qt_ref, k_ref, vt_ref, part_ref, g1_ref,
                     zbuf, a_scr, b_scr, hcar, ccar, *, q_scale):
    tm, d = h_ref.shape[1], h_ref.shape[2]
    n_grp = tm // V7X_SUBLANES
    lru_blk = d // LRU_BLOCKS
    fox_dh = d // FOX_HEADS
    mem_dh = d // MEM_HEADS

    @pl.when(pl.program_id(1) == 0)
    def _():
        zbuf[0:V7X_SUBLANES, :] = jnp.zeros((V7X_SUBLANES, d), F32)
        hcar[...] = jnp.zeros_like(hcar)
        ccar[...] = jnp.zeros_like(ccar)

    u = _rms(h_ref[0], gpre_ref[...]).astype(BF16)

    zbuf[V7X_SUBLANES:V7X_SUBLANES + tm, :] = _dot(u, wlru_ref[...])
    xc = convb_ref[...]
    for kk in range(CONV_W):
        off = V7X_SUBLANES - (CONV_W - 1) + kk
        xc = xc + convw_ref[kk:kk + 1, :] * zbuf[off:off + tm, :]
    zbuf[0:V7X_SUBLANES, :] = zbuf[tm:tm + V7X_SUBLANES, :]
    xcb = xc.astype(BF16)
    neg_c_sp = -LRU_C * _softplus(-lam_ref[...])
    for hb in range(LRU_BLOCKS):
        sl = slice(hb * lru_blk, (hb + 1) * lru_blk)
        ri = _dot(xcb[:, sl], wri_ref[hb])
        r = _sigmoid(ri[:, :lru_blk] + br_ref[:, sl])
        i = _sigmoid(ri[:, lru_blk:] + bi_ref[:, sl])
        a = jnp.exp(r * neg_c_sp[:, sl])
        a_scr[:, sl] = a
        b_scr[:, sl] = jnp.sqrt(1.0 - a * a) * i * xc[:, sl]
    a3 = a_scr[...].reshape(n_grp, V7X_SUBLANES, d)
    b3 = b_scr[...].reshape(n_grp, V7X_SUBLANES, d)
    sub = lax.broadcasted_iota(jnp.int32, a3.shape, 1)
    step = 1
    while step < V7X_SUBLANES:
        keep = sub >= step
        a_prev = jnp.where(keep, pltpu.roll(a3, step, 1), 1.0)
        b_prev = jnp.where(keep, pltpu.roll(b3, step, 1), 0.0)
        b3 = a3 * b_prev + b3
        a3 = a3 * a_prev
        step *= 2
    carry = hcar[...]
    for g in range(n_grp):
        hg = a3[g] * carry + b3[g]
        b_scr[g * V7X_SUBLANES:(g + 1) * V7X_SUBLANES, :] = hg
        carry = hg[V7X_SUBLANES - 1:V7X_SUBLANES, :]
    hcar[...] = carry
    pa = _dot(b_scr[...].astype(BF16), wpa_ref[...])

    kz = _dot(u, wk_ref[...]).astype(BF16)
    zt = _dot_nt(wqvt_ref[...], u)
    qt = (zt[:d] * q_scale).astype(BF16)
    vt = zt[d:].astype(BF16)
    zf = _dot(u, wf_ref[...]) + bf_ref[...]
    lf = (jnp.minimum(zf, 0.0) - jnp.log1p(jnp.exp(-jnp.abs(zf)))) * LOG2E
    c3 = lf.reshape(n_grp, V7X_SUBLANES, BIAS_LANES)
    sub = lax.broadcasted_iota(jnp.int32, c3.shape, 1)
    step = 1
    while step < V7X_SUBLANES:
        c3 = c3 + jnp.where(sub >= step, pltpu.roll(c3, step, 1), 0.0)
        step *= 2
    ccarry = ccar[...]
    c_rows = []
    for g in range(n_grp):
        cg = c3[g] + ccarry
        c_rows.append(cg)
        ccarry = cg[V7X_SUBLANES - 1:V7X_SUBLANES, :]
    ccar[...] = ccarry
    c = jnp.concatenate(c_rows, axis=0)
    c_hi = c.astype(BF16).astype(F32)
    c_mid = (c - c_hi).astype(BF16).astype(F32)
    c_lo = (c - c_hi - c_mid).astype(BF16).astype(F32)
    lane = lax.broadcasted_iota(jnp.int32, c.shape, 1)
    piece = lane % N_SPLIT
    c_split = jnp.where(piece == 0, c_hi, jnp.where(piece == 1, c_mid, c_lo))
    n_bias = FOX_HEADS * N_SPLIT
    k_bias = jnp.where(lane < n_bias, -c_split, jnp.where(lane < 2 * n_bias, 1.0, 0.0)).astype(BF16)
    c_split_t = pltpu.roll(c_split.T, n_bias, 0)
    row = lax.broadcasted_iota(jnp.int32, c_split_t.shape, 0)
    for hh in range(FOX_HEADS):
        sl = slice(hh * fox_dh, (hh + 1) * fox_dh)
        lo = hh * N_SPLIT
        ones_rows = (row >= lo) & (row < lo + N_SPLIT)
        c_rows_sel = (row >= n_bias + lo) & (row < n_bias + lo + N_SPLIT)
        q_bias = jnp.where(ones_rows, 1.0, jnp.where(c_rows_sel, c_split_t, 0.0)).astype(BF16)
        k_ref[0, hh, 0, :, 0:fox_dh] = kz[:, sl]
        k_ref[0, hh, 0, :, fox_dh:fox_dh + BIAS_LANES] = k_bias
        qt_ref[0, hh, 0:fox_dh, :] = qt[sl, :]
        qt_ref[0, hh, fox_dh:fox_dh + BIAS_LANES, :] = q_bias
        vt_ref[0, hh, 0] = vt[sl, :]

    mq = (_dot(u, wmq_ref[...]) * (mem_dh ** -0.5)).astype(BF16)
    yc = []
    for hh in range(MEM_HEADS):
        sl = slice(hh * mem_dh, (hh + 1) * mem_dh)
        lg = _dot_nt(mq[:, sl], kv_ref[0, :, sl])
        e = jnp.exp(lg - jnp.max(lg, axis=-1, keepdims=True))
        num = _dot(e.astype(BF16), kv_ref[0, :, d + hh * mem_dh:d + (hh + 1) * mem_dh])
        yc.append((num / jnp.sum(e, axis=-1, keepdims=True)).astype(BF16))
    pc = _dot(jnp.concatenate(yc, axis=1), wpc_ref[...])

    g0 = _sigmoid(_dot(u, wgate_ref[:, 0:d]) + bgate_ref[0:1, :])
    g2 = _sigmoid(_dot(u, wgate_ref[:, 2 * d:3 * d]) + bgate_ref[2:3, :])
    part_ref[0] = g0 * pa + g2 * pc
    g1_ref[0] = _sigmoid(_dot(u, wgate_ref[:, d:2 * d]) + bgate_ref[1:2, :])


def _mixer_in(h, kv, p):
    bsz, seq, d = h.shape
    tm = ROW_TILE
    n_t = seq // tm
    fox_dh = d // FOX_HEADS
    n_mem, n_kv = kv.shape[1], kv.shape[2]
    qk_dim = fox_dh + BIAS_LANES
    row_spec = pl.BlockSpec((1, tm, d), lambda b, j: (b, j, 0))
    in_specs = [
        row_spec,
        _const_spec((1, d)),
        _const_spec((d, d)), _const_spec((d, d)), _const_spec((2 * d, d)), _const_spec((d, d)),
        _const_spec((d, N_BRANCH * d)), _const_spec((d, BIAS_LANES)),
        _const_spec((CONV_W, d)), _const_spec((1, d)),
        _const_spec((LRU_BLOCKS, d // LRU_BLOCKS, 2 * d // LRU_BLOCKS)),
        _const_spec((1, d)), _const_spec((1, d)), _const_spec((1, d)), _const_spec((1, BIAS_LANES)),
        pl.BlockSpec((1, n_mem, n_kv), lambda b, j: (b, 0, 0)),
        _const_spec((N_BRANCH, d)), _const_spec((d, d)), _const_spec((d, d)),
    ]
    out_specs = [
        pl.BlockSpec((1, FOX_HEADS, qk_dim, tm), lambda b, j: (b, 0, 0, j)),
        pl.BlockSpec((1, FOX_HEADS, 1, tm, qk_dim), lambda b, j: (b, 0, j, 0, 0)),
        pl.BlockSpec((1, FOX_HEADS, 1, fox_dh, tm), lambda b, j: (b, 0, j, 0, 0)),
        row_spec, row_spec,
    ]
    out_shape = [
        jax.ShapeDtypeStruct((bsz, FOX_HEADS, qk_dim, seq), BF16),
        jax.ShapeDtypeStruct((bsz, FOX_HEADS, n_t, tm, qk_dim), BF16),
        jax.ShapeDtypeStruct((bsz, FOX_HEADS, n_t, fox_dh, tm), BF16),
        jax.ShapeDtypeStruct((bsz, seq, d), F32),
        jax.ShapeDtypeStruct((bsz, seq, d), F32),
    ]
    scratch = [
        pltpu.VMEM((tm + V7X_SUBLANES, d), F32),
        pltpu.VMEM((tm, d), F32), pltpu.VMEM((tm, d), F32),
        pltpu.VMEM((1, d), F32), pltpu.VMEM((1, BIAS_LANES), F32),
    ]
    return pl.pallas_call(
        functools.partial(_mixer_in_kernel, q_scale=(fox_dh ** -0.5) * LOG2E),
        grid=(bsz, n_t),
        in_specs=in_specs, out_specs=out_specs, out_shape=out_shape, scratch_shapes=scratch,
        compiler_params=pltpu.CompilerParams(
            dimension_semantics=("arbitrary", "arbitrary"), vmem_limit_bytes=V7X_VMEM_LIMIT_BYTES),
        name="mixer_in",
    )(h, p["g_pre"], p["w_lru"], p["w_k"], p["w_qvt"], p["w_mq"], p["w_gate"], p["w_f"],
      p["conv_w"], p["conv_b"], p["w_ri"], p["b_r"], p["b_i"], p["lam"], p["b_f"], kv,
      p["b_gate"], p["w_pa"], p["w_pc"])


def _fox_kernel(qt_ref, k_ref, vt_ref, o_ref, s0, s1, p0, p1, al0, al1, m_scr, l_scr, acc_scr):
    tq = qt_ref.shape[3]
    tc = k_ref.shape[3]
    per_q = tq // tc
    n_full = pl.program_id(2) * per_q
    n = n_full + per_q
    s_buf, p_buf, al_buf = (s0, s1), (p0, p1), (al0, al1)
    m_scr[...] = jnp.full_like(m_scr, NEG_BIG)
    l_scr[...] = jnp.zeros_like(l_scr)
    acc_scr[...] = jnp.zeros_like(acc_scr)

    def chunk_of(u):
        return jnp.where(u < per_q, n_full + u, u - per_q)

    def scores(c, slot, diag):
        s = _dot(k_ref[0, 0, c], qt_ref[0, 0])
        if diag is not None:
            key = lax.broadcasted_iota(jnp.int32, s.shape, 0) + diag * tc
            qry = lax.broadcasted_iota(jnp.int32, s.shape, 1)
            s = jnp.where(key <= qry, s, NEG_BIG)
        s_buf[slot][...] = s

    def softmax(slot):
        s = s_buf[slot][...]
        m_old = m_scr[...]
        m_new = jnp.maximum(m_old, jnp.max(s, axis=0, keepdims=True))
        p = jnp.exp2(s - m_new)
        alpha = jnp.exp2(m_old - m_new)
        l_scr[...] = alpha * l_scr[...] + jnp.sum(p, axis=0, keepdims=True)
        m_scr[...] = m_new
        p_buf[slot][...] = p.astype(BF16)
        al_buf[slot][...] = alpha

    def accumulate(c, slot):
        acc_scr[...] = al_buf[slot][...] * acc_scr[...] + _dot(vt_ref[0, 0, c], p_buf[slot][...])

    scores(n_full, 0, 0)
    for t in range(per_q - 1):
        scores(n_full + t + 1, (t + 1) % 2, t + 1)
        softmax(t % 2)
        if t >= 1:
            accumulate(n_full + t - 1, (t - 1) % 2)

    def pair(jj, carry):
        t = per_q - 1 + 2 * jj
        scores(2 * jj, 0, None)
        softmax(1)
        accumulate(chunk_of(t - 1), 0)
        scores(2 * jj + 1, 1, None)
        softmax(0)
        accumulate(chunk_of(t), 1)
        return carry

    lax.fori_loop(0, n_full // 2, pair, 0)
    softmax(1)
    accumulate(chunk_of(n - 2), 0)
    accumulate(chunk_of(n - 1), 1)
    o_ref[0] = (acc_scr[...] / l_scr[...]).T.astype(o_ref.dtype)


def _fox_attention(qt, k, vt, d_model):
    bsz, n_heads, qk_dim, seq = qt.shape
    n_c, tc = k.shape[2], k.shape[3]
    dh = vt.shape[3]
    tq = Q_TILE
    assert (tq // tc) % 2 == 0
    return pl.pallas_call(
        _fox_kernel,
        grid=(bsz, n_heads, seq // tq),
        in_specs=[
            pl.BlockSpec((1, 1, qk_dim, tq), lambda b, h, i: (b, h, 0, i)),
            pl.BlockSpec((1, 1, n_c, tc, qk_dim), lambda b, h, i: (b, h, 0, 0, 0)),
            pl.BlockSpec((1, 1, n_c, dh, tc), lambda b, h, i: (b, h, 0, 0, 0)),
        ],
        out_specs=pl.BlockSpec((1, tq, dh), lambda b, h, i: (b, i, h)),
        out_shape=jax.ShapeDtypeStruct((bsz, seq, d_model), BF16),
        scratch_shapes=[
            pltpu.VMEM((tc, tq), F32), pltpu.VMEM((tc, tq), F32),
            pltpu.VMEM((tc, tq), BF16), pltpu.VMEM((tc, tq), BF16),
            pltpu.VMEM((1, tq), F32), pltpu.VMEM((1, tq), F32),
            pltpu.VMEM((1, tq), F32), pltpu.VMEM((1, tq), F32), pltpu.VMEM((dh, tq), F32)],
        compiler_params=pltpu.CompilerParams(
            dimension_semantics=("arbitrary", "arbitrary", "arbitrary"),
            vmem_limit_bytes=V7X_VMEM_LIMIT_BYTES),
        name="fox_attention",
    )(qt, k, vt)


def _post_kernel(yb_ref, part_ref, g1_ref, h_ref, wpb_ref, wo_ref, gpost_ref, gmlp_ref, wup_ref,
                 wdown_ref, gmlp_post_ref, out_ref):
    merged = part_ref[0] + g1_ref[0] * _dot(yb_ref[0], wpb_ref[...])
    h1 = h_ref[0] + _rms(_dot(merged.astype(BF16), wo_ref[...]), gpost_ref[...])
    u = _rms(h1, gmlp_ref[...]).astype(BF16)
    act = jnp.square(jnp.maximum(_dot(u, wup_ref[...]), 0.0))
    out_ref[0] = h1 + _rms(_dot(act.astype(BF16), wdown_ref[...]), gmlp_post_ref[...])


def _post(yb, part, g1, h, p):
    bsz, seq, d = h.shape
    d_ff = p["w_up"].shape[1]
    tm = ROW_TILE
    row_spec = pl.BlockSpec((1, tm, d), lambda b, j: (b, j, 0))
    return pl.pallas_call(
        _post_kernel,
        grid=(bsz, seq // tm),
        in_specs=[row_spec, row_spec, row_spec, row_spec,
                  _const_spec((d, d)), _const_spec((d, d)), _const_spec((1, d)), _const_spec((1, d)),
                  _const_spec((d, d_ff)), _const_spec((d_ff, d)), _const_spec((1, d))],
        out_specs=row_spec,
        out_shape=jax.ShapeDtypeStruct((bsz, seq, d), F32),
        compiler_params=pltpu.CompilerParams(
            dimension_semantics=("arbitrary", "arbitrary"), vmem_limit_bytes=V7X_VMEM_LIMIT_BYTES),
        name="post",
    )(yb, part, g1, h, p["w_pb"], p["w_o"], p["g_post"], p["g_mlp_pre"], p["w_up"], p["w_down"],
      p["g_mlp_post"])


def _layer_params(l, d, g_mix_pre, w_in, conv_w, conv_b, w_r, b_r, w_i, b_i, lru_lambda, b_f,
                  b_gate, w_pa, w_pb, w_pc, w_o, g_mix_post, g_mlp_pre, w_up, w_down, g_mlp_post):
    wi = w_in[l]
    off_q, off_k, off_v, off_mq, off_gate = d, 2 * d, 3 * d, 4 * d, 5 * d
    off_f = off_gate + N_BRANCH * d
    w_f = jnp.repeat(wi[:, off_f:off_f + FOX_HEADS], N_SPLIT, axis=1)
    pad = BIAS_LANES - FOX_HEADS * N_SPLIT
    row = lambda v: v.reshape(1, -1).astype(F32)
    return dict(
        g_pre=row(g_mix_pre[l]),
        w_lru=wi[:, 0:d].astype(BF16),
        w_k=wi[:, off_k:off_k + d].astype(BF16),
        w_qvt=jnp.concatenate([wi[:, off_q:off_q + d].T, wi[:, off_v:off_v + d].T], axis=0).astype(BF16),
        w_mq=wi[:, off_mq:off_mq + d].astype(BF16),
        w_gate=wi[:, off_gate:off_gate + N_BRANCH * d].astype(BF16),
        w_f=jnp.pad(w_f, ((0, 0), (0, pad))).astype(BF16),
        conv_w=conv_w[l].astype(F32), conv_b=row(conv_b[l]),
        w_ri=jnp.concatenate([w_r[l], w_i[l]], axis=-1).astype(BF16),
        b_r=row(b_r[l]), b_i=row(b_i[l]), lam=row(lru_lambda[l]),
        b_f=jnp.pad(jnp.repeat(b_f[l], N_SPLIT), (0, pad)).reshape(1, -1).astype(F32),
        b_gate=b_gate[l].astype(F32),
        w_pa=w_pa[l].astype(BF16), w_pb=w_pb[l].astype(BF16), w_pc=w_pc[l].astype(BF16),
        w_o=w_o[l].astype(BF16), g_post=row(g_mix_post[l]), g_mlp_pre=row(g_mlp_pre[l]),
        w_up=w_up[l].astype(BF16), w_down=w_down[l].astype(BF16), g_mlp_post=row(g_mlp_post[l]),
    )


def kernel(x, mem, g_mix_pre, w_in, conv_w, conv_b, w_r, b_r, w_i, b_i, lru_lambda, b_f, g_mem, w_mem_kv, b_gate, w_pa, w_pb, w_pc, w_o, g_mix_post, g_mlp_pre, w_up, w_down, g_mlp_post):
    bsz, seq, d = x.shape
    depth = w_in.shape[0]
    assert seq % Q_TILE == 0 and Q_TILE % ROW_TILE == 0 and d % (FOX_HEADS * V7X_LANES) == 0
    kv_all = _mem_kv(mem, g_mem, w_mem_kv)
    h = x
    for l in range(depth):
        p = _layer_params(l, d, g_mix_pre, w_in, conv_w, conv_b, w_r, b_r, w_i, b_i, lru_lambda,
                          b_f, b_gate, w_pa, w_pb, w_pc, w_o, g_mix_post, g_mlp_pre, w_up, w_down,
                          g_mlp_post)
        qt, k, vt, part, g1 = _mixer_in(h, kv_all[l], p)
        yb = _fox_attention(qt, k, vt, d)
        h = _post(yb, part, g1, h, p)
    return h
```

```python
import functools

import jax
import jax.numpy as jnp
from jax import lax
from jax.experimental import pallas as pl
from jax.experimental.pallas import tpu as pltpu

EPS = 1e-6
LRU_BLOCKS = 8
LRU_C = 8.0
CONV_W = 4
FOX_HEADS = 8
MEM_HEADS = 4
N_BRANCH = 3
LOG2E = 1.4426950408889634
NEG_BIG = -1e30

V7X_LANES = 128
V7X_SUBLANES = 8
V7X_BF16_SUBLANES = 16
V7X_VMEM_LIMIT_BYTES = 56 * 1024 * 1024

ROW_TILE = 256
Q_TILE = 1024
BIAS_LANES = V7X_LANES
N_SPLIT = 3
REF_LANE0 = 2 * FOX_HEADS * N_SPLIT
REF_SLACK = 1.01
L_UNDERFLOW = 2.0 ** -60
assert REF_LANE0 % V7X_BF16_SUBLANES == 0 and REF_LANE0 + N_SPLIT <= BIAS_LANES

F32 = jnp.float32
BF16 = jnp.bfloat16


def _const_spec(shape):
    nd = len(shape)
    return pl.BlockSpec(shape, lambda *_: (0,) * nd, pipeline_mode=pl.Buffered(1))


def _rms(x, g):
    return x * lax.rsqrt(jnp.mean(x * x, axis=-1, keepdims=True) + EPS) * g


def _sigmoid(x):
    return 1.0 / (1.0 + jnp.exp(-x))


def _softplus(x):
    return jnp.maximum(x, 0.0) + jnp.log1p(jnp.exp(-jnp.abs(x)))


def _dot(a, b):
    return jnp.dot(a, b, preferred_element_type=F32)


def _dot_nt(a, b):
    return lax.dot_general(a, b, (((1,), (1,)), ((), ())), preferred_element_type=F32)


def _mem_kv_kernel(mem_ref, g_ref, w_ref, kv_ref):
    u = _rms(mem_ref[0], g_ref[0]).astype(BF16)
    kv_ref[0, 0] = _dot(u, w_ref[0]).astype(BF16)


def _mem_kv(mem, g_mem, w_mem_kv):
    depth, d_model, n_kv = w_mem_kv.shape
    bsz, n_mem, _ = mem.shape
    return pl.pallas_call(
        _mem_kv_kernel,
        grid=(depth, bsz),
        in_specs=[
            pl.BlockSpec((1, n_mem, d_model), lambda l, b: (b, 0, 0)),
            pl.BlockSpec((1, 1, d_model), lambda l, b: (l, 0, 0)),
            pl.BlockSpec((1, d_model, n_kv), lambda l, b: (l, 0, 0)),
        ],
        out_specs=pl.BlockSpec((1, 1, n_mem, n_kv), lambda l, b: (l, b, 0, 0)),
        out_shape=jax.ShapeDtypeStruct((depth, bsz, n_mem, n_kv), BF16),
        compiler_params=pltpu.CompilerParams(
            dimension_semantics=("arbitrary", "arbitrary"), vmem_limit_bytes=V7X_VMEM_LIMIT_BYTES),
        name="mem_kv",
    )(mem, g_mem.reshape(depth, 1, d_model), w_mem_kv.astype(BF16))


def _mixer_in_kernel(h_ref, gpre_ref, wlru_ref, wk_ref, wqvt_ref, wmq_ref, wgate_ref, wf_ref,
                     convw_ref, convb_ref, wri_ref, br_ref, bi_ref, lam_ref, bf_ref, kv_ref,
                     bgate_ref, wpa_ref, wpc_ref,
                     qt_ref, k_ref, vt_ref, part_ref, g1_ref,
                     zbuf, a_scr, b_scr, hcar, ccar, *, q_scale):
    tm, d = h_ref.shape[1], h_ref.shape[2]
    n_grp = tm // V7X_SUBLANES
    lru_blk = d // LRU_BLOCKS
    fox_dh = d // FOX_HEADS
    mem_dh = d // MEM_HEADS

    @pl.when(pl.program_id(1) == 0)
    def _():
        zbuf[0:V7X_SUBLANES, :] = jnp.zeros((V7X_SUBLANES, d), F32)
        hcar[...] = jnp.zeros_like(hcar)
        ccar[...] = jnp.zeros_like(ccar)

    u = _rms(h_ref[0], gpre_ref[...]).astype(BF16)

    zbuf[V7X_SUBLANES:V7X_SUBLANES + tm, :] = _dot(u, wlru_ref[...])
    xc = convb_ref[...]
    for kk in range(CONV_W):
        off = V7X_SUBLANES - (CONV_W - 1) + kk
        xc = xc + convw_ref[kk:kk + 1, :] * zbuf[off:off + tm, :]
    zbuf[0:V7X_SUBLANES, :] = zbuf[tm:tm + V7X_SUBLANES, :]
    xcb = xc.astype(BF16)
    neg_c_sp = -LRU_C * _softplus(-lam_ref[...])
    for hb in range(LRU_BLOCKS):
        sl = slice(hb * lru_blk, (hb + 1) * lru_blk)
        ri = _dot(xcb[:, sl], wri_ref[hb])
        r = _sigmoid(ri[:, :lru_blk] + br_ref[:, sl])
        i = _sigmoid(ri[:, lru_blk:] + bi_ref[:, sl])
        a = jnp.exp(r * neg_c_sp[:, sl])
        a_scr[:, sl] = a
        b_scr[:, sl] = jnp.sqrt(1.0 - a * a) * i * xc[:, sl]
    a3 = a_scr[...].reshape(n_grp, V7X_SUBLANES, d)
    b3 = b_scr[...].reshape(n_grp, V7X_SUBLANES, d)
    sub = lax.broadcasted_iota(jnp.int32, a3.shape, 1)
    step = 1
    while step < V7X_SUBLANES:
        keep = sub >= step
        a_prev = jnp.where(keep, pltpu.roll(a3, step, 1), 1.0)
        b_prev = jnp.where(keep, pltpu.roll(b3, step, 1), 0.0)
        b3 = a3 * b_prev + b3
        a3 = a3 * a_prev
        step *= 2
    carry = hcar[...]
    for g in range(n_grp):
        hg = a3[g] * carry + b3[g]
        b_scr[g * V7X_SUBLANES:(g + 1) * V7X_SUBLANES, :] = hg
        carry = hg[V7X_SUBLANES - 1:V7X_SUBLANES, :]
    hcar[...] = carry
    pa = _dot(b_scr[...].astype(BF16), wpa_ref[...])

    kz = _dot(u, wk_ref[...]).astype(BF16)
    zt = _dot_nt(wqvt_ref[...], u)
    qt = (zt[:d] * q_scale).astype(BF16)
    vt = zt[d:].astype(BF16)
    zf = _dot(u, wf_ref[...]) + bf_ref[...]
    lf = (jnp.minimum(zf, 0.0) - jnp.log1p(jnp.exp(-jnp.abs(zf)))) * LOG2E
    c3 = lf.reshape(n_grp, V7X_SUBLANES, BIAS_LANES)
    sub = lax.broadcasted_iota(jnp.int32, c3.shape, 1)
    step = 1
    while step < V7X_SUBLANES:
        c3 = c3 + jnp.where(sub >= step, pltpu.roll(c3, step, 1), 0.0)
        step *= 2
    ccarry = ccar[...]
    c_rows = []
    for g in range(n_grp):
        cg = c3[g] + ccarry
        c_rows.append(cg)
        ccarry = cg[V7X_SUBLANES - 1:V7X_SUBLANES, :]
    ccar[...] = ccarry
    c = jnp.concatenate(c_rows, axis=0)
    c_hi = c.astype(BF16).astype(F32)
    c_mid = (c - c_hi).astype(BF16).astype(F32)
    c_lo = (c - c_hi - c_mid).astype(BF16).astype(F32)
    lane = lax.broadcasted_iota(jnp.int32, c.shape, 1)
    piece = lane % N_SPLIT
    c_split = jnp.where(piece == 0, c_hi, jnp.where(piece == 1, c_mid, c_lo))
    n_bias = FOX_HEADS * N_SPLIT
    k_bias = jnp.where(lane < n_bias, -c_split,
                       jnp.where(lane < REF_LANE0 + N_SPLIT, 1.0, 0.0)).astype(BF16)
    c_split_t = pltpu.roll(c_split.T, n_bias, 0)
    row = lax.broadcasted_iota(jnp.int32, c_split_t.shape, 0)
    for hh in range(FOX_HEADS):
        sl = slice(hh * fox_dh, (hh + 1) * fox_dh)
        lo = hh * N_SPLIT
        ones_rows = (row >= lo) & (row < lo + N_SPLIT)
        c_rows_sel = (row >= n_bias + lo) & (row < n_bias + lo + N_SPLIT)
        q_bias = jnp.where(ones_rows, 1.0, jnp.where(c_rows_sel, c_split_t, 0.0)).astype(BF16)
        k_ref[0, hh, 0, :, 0:fox_dh] = kz[:, sl]
        k_ref[0, hh, 0, :, fox_dh:fox_dh + BIAS_LANES] = k_bias
        qt_ref[0, hh, 0:fox_dh, :] = qt[sl, :]
        qt_ref[0, hh, fox_dh:fox_dh + BIAS_LANES, :] = q_bias
        vt_ref[0, hh, 0] = vt[sl, :]

    mq = (_dot(u, wmq_ref[...]) * (mem_dh ** -0.5)).astype(BF16)
    yc = []
    for hh in range(MEM_HEADS):
        sl = slice(hh * mem_dh, (hh + 1) * mem_dh)
        lg = _dot_nt(mq[:, sl], kv_ref[0, :, sl])
        e = jnp.exp(lg - jnp.max(lg, axis=-1, keepdims=True))
        num = _dot(e.astype(BF16), kv_ref[0, :, d + hh * mem_dh:d + (hh + 1) * mem_dh])
        yc.append((num / jnp.sum(e, axis=-1, keepdims=True)).astype(BF16))
    pc = _dot(jnp.concatenate(yc, axis=1), wpc_ref[...])

    g0 = _sigmoid(_dot(u, wgate_ref[:, 0:d]) + bgate_ref[0:1, :])
    g2 = _sigmoid(_dot(u, wgate_ref[:, 2 * d:3 * d]) + bgate_ref[2:3, :])
    part_ref[0] = g0 * pa + g2 * pc
    g1_ref[0] = _sigmoid(_dot(u, wgate_ref[:, d:2 * d]) + bgate_ref[1:2, :])


def _mixer_in(h, kv, p):
    bsz, seq, d = h.shape
    tm = ROW_TILE
    n_t = seq // tm
    fox_dh = d // FOX_HEADS
    n_mem, n_kv = kv.shape[1], kv.shape[2]
    qk_dim = fox_dh + BIAS_LANES
    row_spec = pl.BlockSpec((1, tm, d), lambda b, j: (b, j, 0))
    in_specs = [
        row_spec,
        _const_spec((1, d)),
        _const_spec((d, d)), _const_spec((d, d)), _const_spec((2 * d, d)), _const_spec((d, d)),
        _const_spec((d, N_BRANCH * d)), _const_spec((d, BIAS_LANES)),
        _const_spec((CONV_W, d)), _const_spec((1, d)),
        _const_spec((LRU_BLOCKS, d // LRU_BLOCKS, 2 * d // LRU_BLOCKS)),
        _const_spec((1, d)), _const_spec((1, d)), _const_spec((1, d)), _const_spec((1, BIAS_LANES)),
        pl.BlockSpec((1, n_mem, n_kv), lambda b, j: (b, 0, 0)),
        _const_spec((N_BRANCH, d)), _const_spec((d, d)), _const_spec((d, d)),
    ]
    out_specs = [
        pl.BlockSpec((1, FOX_HEADS, qk_dim, tm), lambda b, j: (b, 0, 0, j)),
        pl.BlockSpec((1, FOX_HEADS, 1, tm, qk_dim), lambda b, j: (b, 0, j, 0, 0)),
        pl.BlockSpec((1, FOX_HEADS, 1, fox_dh, tm), lambda b, j: (b, 0, j, 0, 0)),
        row_spec, row_spec,
    ]
    out_shape = [
        jax.ShapeDtypeStruct((bsz, FOX_HEADS, qk_dim, seq), BF16),
        jax.ShapeDtypeStruct((bsz, FOX_HEADS, n_t, tm, qk_dim), BF16),
        jax.ShapeDtypeStruct((bsz, FOX_HEADS, n_t, fox_dh, tm), BF16),
        jax.ShapeDtypeStruct((bsz, seq, d), F32),
        jax.ShapeDtypeStruct((bsz, seq, d), F32),
    ]
    scratch = [
        pltpu.VMEM((tm + V7X_SUBLANES, d), F32),
        pltpu.VMEM((tm, d), F32), pltpu.VMEM((tm, d), F32),
        pltpu.VMEM((1, d), F32), pltpu.VMEM((1, BIAS_LANES), F32),
    ]
    return pl.pallas_call(
        functools.partial(_mixer_in_kernel, q_scale=(fox_dh ** -0.5) * LOG2E),
        grid=(bsz, n_t),
        in_specs=in_specs, out_specs=out_specs, out_shape=out_shape, scratch_shapes=scratch,
        compiler_params=pltpu.CompilerParams(
            dimension_semantics=("arbitrary", "arbitrary"), vmem_limit_bytes=V7X_VMEM_LIMIT_BYTES),
        name="mixer_in",
    )(h, p["g_pre"], p["w_lru"], p["w_k"], p["w_qvt"], p["w_mq"], p["w_gate"], p["w_f"],
      p["conv_w"], p["conv_b"], p["w_ri"], p["b_r"], p["b_i"], p["lam"], p["b_f"], kv,
      p["b_gate"], p["w_pa"], p["w_pc"])


def _fox_kernel(qt_ref, k_ref, vt_ref, o_ref, q_eff, p0, p1, acc_scr, l_scr, m_scr, kmax_scr):
    tq = qt_ref.shape[3]
    n_c, tc = k_ref.shape[2], k_ref.shape[3]
    dh = o_ref.shape[2]
    per_q = tq // tc
    i = pl.program_id(2)
    p_buf = (p0, p1)
    ref_row0 = dh + REF_LANE0

    @pl.when(i == 0)
    def _():
        def body(c, mx):
            kk = k_ref[0, 0, c, :, 0:dh].astype(F32)
            return jnp.maximum(mx, jnp.sum(kk * kk, axis=1, keepdims=True))
        mx = lax.fori_loop(0, n_c, body, jnp.zeros((tc, 1), F32))
        kmax_scr[...] = jnp.max(mx, axis=0, keepdims=True)

    q = qt_ref[0, 0]
    qf = q[0:dh].astype(F32)
    ref = jnp.sqrt(jnp.sum(qf * qf, axis=0, keepdims=True) * kmax_scr[...]) * REF_SLACK
    r_hi = ref.astype(BF16).astype(F32)
    r_mid = (ref - r_hi).astype(BF16).astype(F32)
    r_lo = (ref - r_hi - r_mid).astype(BF16).astype(F32)
    row = lax.broadcasted_iota(jnp.int32, (V7X_BF16_SUBLANES, tq), 0)
    q_eff[...] = q
    q_eff[ref_row0:ref_row0 + V7X_BF16_SUBLANES, :] = jnp.where(
        row == 0, -r_hi, jnp.where(row == 1, -r_mid, jnp.where(row == 2, -r_lo, 0.0))).astype(BF16)
    acc_scr[...] = jnp.zeros_like(acc_scr)
    l_scr[...] = jnp.zeros_like(l_scr)

    def probs(tile, slot, masked):
        l_new = l_scr[...]
        for sub in range(per_q):
            s = _dot(k_ref[0, 0, tile * per_q + sub], q_eff[...])
            if masked:
                key = lax.broadcasted_iota(jnp.int32, s.shape, 0) + sub * tc
                qry = lax.broadcasted_iota(jnp.int32, s.shape, 1)
                s = jnp.where(key <= qry, s, NEG_BIG)
            p = jnp.exp2(s)
            l_new = l_new + jnp.sum(p, axis=0, keepdims=True)
            p_buf[slot][sub * tc:(sub + 1) * tc, :] = p.astype(BF16)
        l_scr[...] = l_new

    def accumulate(tile, slot):
        upd = _dot(vt_ref[0, 0, tile * per_q], p_buf[slot][0:tc, :])
        for sub in range(1, per_q):
            upd = upd + _dot(vt_ref[0, 0, tile * per_q + sub], p_buf[slot][sub * tc:(sub + 1) * tc, :])
        acc_scr[...] += upd

    def by_parity(x, fn):
        for par in range(2):
            pl.when(x % 2 == par)(functools.partial(fn, par))

    def tile_of(v):
        return jnp.where(v == 0, i, v - 1)

    probs(i, 0, True)

    def step(t, carry):
        def one(par):
            probs(t, 1 - par, False)
            accumulate(tile_of(t), par)
        by_parity(t, one)
        return carry

    lax.fori_loop(0, i, step, 0)
    by_parity(i, lambda par: accumulate(tile_of(i), par))

    @pl.when(jnp.logical_not(jnp.min(l_scr[...]) >= L_UNDERFLOW))
    def _():
        m_scr[...] = jnp.full_like(m_scr, NEG_BIG)
        l_scr[...] = jnp.zeros_like(l_scr)
        acc_scr[...] = jnp.zeros_like(acc_scr)

        def chunk(c, carry):
            s = _dot(k_ref[0, 0, c], qt_ref[0, 0])
            key = lax.broadcasted_iota(jnp.int32, s.shape, 0) + c * tc
            qry = lax.broadcasted_iota(jnp.int32, s.shape, 1) + i * tq
            s = jnp.where(key <= qry, s, NEG_BIG)
            m_old = m_scr[...]
            m_new = jnp.maximum(m_old, jnp.max(s, axis=0, keepdims=True))
            p = jnp.exp2(s - m_new)
            alpha = jnp.exp2(m_old - m_new)
            l_scr[...] = alpha * l_scr[...] + jnp.sum(p, axis=0, keepdims=True)
            acc_scr[...] = alpha * acc_scr[...] + _dot(vt_ref[0, 0, c], p.astype(BF16))
            m_scr[...] = m_new
            return carry

        lax.fori_loop(0, (i + 1) * per_q, chunk, 0)

    o_ref[0] = (acc_scr[...] / l_scr[...]).T.astype(o_ref.dtype)


def _fox_attention(qt, k, vt, d_model):
    bsz, n_heads, qk_dim, seq = qt.shape
    n_c, tc = k.shape[2], k.shape[3]
    dh = vt.shape[3]
    tq = Q_TILE
    resident = lambda shape: pl.BlockSpec(shape, lambda b, h, i: (b, h, 0, 0, 0))
    return pl.pallas_call(
        _fox_kernel,
        grid=(bsz, n_heads, seq // tq),
        in_specs=[
            pl.BlockSpec((1, 1, qk_dim, tq), lambda b, h, i: (b, h, 0, i)),
            resident((1, 1, n_c, tc, qk_dim)),
            resident((1, 1, n_c, dh, tc)),
        ],
        out_specs=pl.BlockSpec((1, tq, dh), lambda b, h, i: (b, i, h)),
        out_shape=jax.ShapeDtypeStruct((bsz, seq, d_model), BF16),
        scratch_shapes=[
            pltpu.VMEM((qk_dim, tq), BF16),
            pltpu.VMEM((tq, tq), BF16), pltpu.VMEM((tq, tq), BF16),
            pltpu.VMEM((dh, tq), F32), pltpu.VMEM((1, tq), F32),
            pltpu.VMEM((1, tq), F32),
            pltpu.VMEM((1, 1), F32)],
        compiler_params=pltpu.CompilerParams(
            dimension_semantics=("arbitrary", "arbitrary", "arbitrary"),
            vmem_limit_bytes=V7X_VMEM_LIMIT_BYTES),
        name="fox_attention",
    )(qt, k, vt)


def _post_kernel(yb_ref, part_ref, g1_ref, h_ref, wpb_ref, wo_ref, gpost_ref, gmlp_ref, wup_ref,
                 wdown_ref, gmlp_post_ref, out_ref):
    merged = part_ref[0] + g1_ref[0] * _dot(yb_ref[0], wpb_ref[...])
    h1 = h_ref[0] + _rms(_dot(merged.astype(BF16), wo_ref[...]), gpost_ref[...])
    u = _rms(h1, gmlp_ref[...]).astype(BF16)
    act = jnp.square(jnp.maximum(_dot(u, wup_ref[...]), 0.0))
    out_ref[0] = h1 + _rms(_dot(act.astype(BF16), wdown_ref[...]), gmlp_post_ref[...])


def _post(yb, part, g1, h, p):
    bsz, seq, d = h.shape
    d_ff = p["w_up"].shape[1]
    tm = ROW_TILE
    row_spec = pl.BlockSpec((1, tm, d), lambda b, j: (b, j, 0))
    return pl.pallas_call(
        _post_kernel,
        grid=(bsz, seq // tm),
        in_specs=[row_spec, row_spec, row_spec, row_spec,
                  _const_spec((d, d)), _const_spec((d, d)), _const_spec((1, d)), _const_spec((1, d)),
                  _const_spec((d, d_ff)), _const_spec((d_ff, d)), _const_spec((1, d))],
        out_specs=row_spec,
        out_shape=jax.ShapeDtypeStruct((bsz, seq, d), F32),
        compiler_params=pltpu.CompilerParams(
            dimension_semantics=("arbitrary", "arbitrary"), vmem_limit_bytes=V7X_VMEM_LIMIT_BYTES),
        name="post",
    )(yb, part, g1, h, p["w_pb"], p["w_o"], p["g_post"], p["g_mlp_pre"], p["w_up"], p["w_down"],
      p["g_mlp_post"])


def _layer_params(l, d, g_mix_pre, w_in, conv_w, conv_b, w_r, b_r, w_i, b_i, lru_lambda, b_f,
                  b_gate, w_pa, w_pb, w_pc, w_o, g_mix_post, g_mlp_pre, w_up, w_down, g_mlp_post):
    wi = w_in[l]
    off_q, off_k, off_v, off_mq, off_gate = d, 2 * d, 3 * d, 4 * d, 5 * d
    off_f = off_gate + N_BRANCH * d
    w_f = jnp.repeat(wi[:, off_f:off_f + FOX_HEADS], N_SPLIT, axis=1)
    pad = BIAS_LANES - FOX_HEADS * N_SPLIT
    row = lambda v: v.reshape(1, -1).astype(F32)
    return dict(
        g_pre=row(g_mix_pre[l]),
        w_lru=wi[:, 0:d].astype(BF16),
        w_k=wi[:, off_k:off_k + d].astype(BF16),
        w_qvt=jnp.concatenate([wi[:, off_q:off_q + d].T, wi[:, off_v:off_v + d].T], axis=0).astype(BF16),
        w_mq=wi[:, off_mq:off_mq + d].astype(BF16),
        w_gate=wi[:, off_gate:off_gate + N_BRANCH * d].astype(BF16),
        w_f=jnp.pad(w_f, ((0, 0), (0, pad))).astype(BF16),
        conv_w=conv_w[l].astype(F32), conv_b=row(conv_b[l]),
        w_ri=jnp.concatenate([w_r[l], w_i[l]], axis=-1).astype(BF16),
        b_r=row(b_r[l]), b_i=row(b_i[l]), lam=row(lru_lambda[l]),
        b_f=jnp.pad(jnp.repeat(b_f[l], N_SPLIT), (0, pad)).reshape(1, -1).astype(F32),
        b_gate=b_gate[l].astype(F32),
        w_pa=w_pa[l].astype(BF16), w_pb=w_pb[l].astype(BF16), w_pc=w_pc[l].astype(BF16),
        w_o=w_o[l].astype(BF16), g_post=row(g_mix_post[l]), g_mlp_pre=row(g_mlp_pre[l]),
        w_up=w_up[l].astype(BF16), w_down=w_down[l].astype(BF16), g_mlp_post=row(g_mlp_post[l]),
    )


def kernel(x, mem, g_mix_pre, w_in, conv_w, conv_b, w_r, b_r, w_i, b_i, lru_lambda, b_f, g_mem, w_mem_kv, b_gate, w_pa, w_pb, w_pc, w_o, g_mix_post, g_mlp_pre, w_up, w_down, g_mlp_post):
    bsz, seq, d = x.shape
    depth = w_in.shape[0]
    assert seq % Q_TILE == 0 and Q_TILE % ROW_TILE == 0 and d % (FOX_HEADS * V7X_LANES) == 0
    kv_all = _mem_kv(mem, g_mem, w_mem_kv)
    h = x
    for l in range(depth):
        p = _layer_params(l, d, g_mix_pre, w_in, conv_w, conv_b, w_r, b_r, w_i, b_i, lru_lambda,
                          b_f, b_gate, w_pa, w_pb, w_pc, w_o, g_mix_post, g_mlp_pre, w_up, w_down,
                          g_mlp_post)
        qt, k, vt, part, g1 = _mixer_in(h, kv_all[l], p)
        yb = _fox_attention(qt, k, vt, d)
        h = _post(yb, part, g1, h, p)
    return h
```

```python
import functools

import jax
import jax.numpy as jnp
from jax import lax
from jax.experimental import pallas as pl
from jax.experimental.pallas import tpu as pltpu

EPS = 1e-6
LRU_BLOCKS = 8
LRU_C = 8.0
CONV_W = 4
FOX_HEADS = 8
MEM_HEADS = 4
N_BRANCH = 3
LOG2E = 1.4426950408889634
NEG_BIG = -1e30

V7X_LANES = 128
V7X_SUBLANES = 8
V7X_BF16_SUBLANES = 16
V7X_VMEM_LIMIT_BYTES = 56 * 1024 * 1024

ROW_TILE = 256
Q_TILE = 1024
JOB_TILES = 2
BIAS_LANES = V7X_LANES
N_SPLIT = 3
REF_LANE0 = 2 * FOX_HEADS * N_SPLIT
REF_SLACK = 1.01
L_UNDERFLOW = 2.0 ** -60
assert REF_LANE0 % V7X_BF16_SUBLANES == 0 and REF_LANE0 + N_SPLIT <= BIAS_LANES

F32 = jnp.float32
BF16 = jnp.bfloat16


def _const_spec(shape):
    nd = len(shape)
    return pl.BlockSpec(shape, lambda *_: (0,) * nd, pipeline_mode=pl.Buffered(1))


def _rms(x, g):
    return x * lax.rsqrt(jnp.mean(x * x, axis=-1, keepdims=True) + EPS) * g


def _sigmoid(x):
    return 1.0 / (1.0 + jnp.exp2(x * (-LOG2E)))


def _softplus(x):
    return jnp.maximum(x, 0.0) + jnp.log1p(jnp.exp(-jnp.abs(x)))


def _dot(a, b):
    return jnp.dot(a, b, preferred_element_type=F32)


def _dot_nt(a, b):
    return lax.dot_general(a, b, (((1,), (1,)), ((), ())), preferred_element_type=F32)


def _mem_kv_kernel(mem_ref, g_ref, w_ref, kv_ref):
    u = _rms(mem_ref[0], g_ref[0]).astype(BF16)
    kv_ref[0, 0] = _dot(u, w_ref[0]).astype(BF16)


def _mem_kv(mem, g_mem, w_mem_kv):
    depth, d_model, n_kv = w_mem_kv.shape
    bsz, n_mem, _ = mem.shape
    return pl.pallas_call(
        _mem_kv_kernel,
        grid=(depth, bsz),
        in_specs=[
            pl.BlockSpec((1, n_mem, d_model), lambda l, b: (b, 0, 0)),
            pl.BlockSpec((1, 1, d_model), lambda l, b: (l, 0, 0)),
            pl.BlockSpec((1, d_model, n_kv), lambda l, b: (l, 0, 0)),
        ],
        out_specs=pl.BlockSpec((1, 1, n_mem, n_kv), lambda l, b: (l, b, 0, 0)),
        out_shape=jax.ShapeDtypeStruct((depth, bsz, n_mem, n_kv), BF16),
        compiler_params=pltpu.CompilerParams(
            dimension_semantics=("arbitrary", "arbitrary"), vmem_limit_bytes=V7X_VMEM_LIMIT_BYTES),
        name="mem_kv",
    )(mem, g_mem.reshape(depth, 1, d_model), w_mem_kv.astype(BF16))


def _mixer_in_kernel(h_ref, gpre_ref, wlru_ref, wk_ref, wqvt_ref, wmq_ref, wgate_ref, wf_ref,
                     convw_ref, convb_ref, wri_ref, br_ref, bi_ref, lam_ref, bf_ref, kv_ref,
                     bgate_ref, wpa_ref, wpc_ref,
                     qt_ref, k_ref, vt_ref, part_ref, g1_ref,
                     conv_car, a_scr, b_scr, hcar, ccar, *, q_scale):
    tm, d = h_ref.shape[1], h_ref.shape[2]
    n_grp = tm // V7X_SUBLANES
    lru_blk = d // LRU_BLOCKS
    fox_dh = d // FOX_HEADS
    mem_dh = d // MEM_HEADS

    @pl.when(pl.program_id(1) == 0)
    def _():
        conv_car[...] = jnp.zeros_like(conv_car)
        hcar[...] = jnp.zeros_like(hcar)
        ccar[...] = jnp.zeros_like(ccar)

    u = _rms(h_ref[0], gpre_ref[...]).astype(BF16)

    z = _dot(u, wlru_ref[...])
    first_row = lax.broadcasted_iota(jnp.int32, z.shape, 0) == 0
    y = convw_ref[0:1, :] * z
    for kk in range(1, CONV_W):
        rot = pltpu.roll(y, 1, 0)
        y = jnp.where(first_row, conv_car[kk - 1:kk, :], rot) + convw_ref[kk:kk + 1, :] * z
        conv_car[kk - 1:kk, :] = rot[0:1, :]
    xc = y + convb_ref[...]
    xcb = xc.astype(BF16)
    neg_c_sp = (-LRU_C * LOG2E) * _softplus(-lam_ref[...])
    for hb in range(LRU_BLOCKS):
        sl = slice(hb * lru_blk, (hb + 1) * lru_blk)
        ri = _dot(xcb[:, sl], wri_ref[hb])
        r = _sigmoid(ri[:, :lru_blk] + br_ref[:, sl])
        i = _sigmoid(ri[:, lru_blk:] + bi_ref[:, sl])
        a = jnp.exp2(r * neg_c_sp[:, sl])
        a_scr[:, sl] = a
        b_scr[:, sl] = jnp.exp2(0.5 * jnp.log2(1.0 - a * a)) * i * xc[:, sl]
    a3 = a_scr[...].reshape(n_grp, V7X_SUBLANES, d)
    b3 = b_scr[...].reshape(n_grp, V7X_SUBLANES, d)
    sub = lax.broadcasted_iota(jnp.int32, a3.shape, 1)
    step = 1
    while step < V7X_SUBLANES:
        keep = sub >= step
        a_prev = jnp.where(keep, pltpu.roll(a3, step, 1), 1.0)
        b_prev = jnp.where(keep, pltpu.roll(b3, step, 1), 0.0)
        b3 = a3 * b_prev + b3
        a3 = a3 * a_prev
        step *= 2
    carry = hcar[...]
    for g in range(n_grp):
        hg = a3[g] * carry + b3[g]
        b_scr[g * V7X_SUBLANES:(g + 1) * V7X_SUBLANES, :] = hg
        carry = hg[V7X_SUBLANES - 1:V7X_SUBLANES, :]
    hcar[...] = carry
    pa = _dot(b_scr[...].astype(BF16), wpa_ref[...])

    kz = _dot(u, wk_ref[...]).astype(BF16)
    zt = _dot_nt(wqvt_ref[...], u)
    qt = (zt[:d] * q_scale).astype(BF16)
    vt = zt[d:].astype(BF16)
    zf = _dot(u, wf_ref[...]) + bf_ref[...]
    lf = (jnp.minimum(zf, 0.0) - jnp.log1p(jnp.exp(-jnp.abs(zf)))) * LOG2E
    c3 = lf.reshape(n_grp, V7X_SUBLANES, BIAS_LANES)
    sub = lax.broadcasted_iota(jnp.int32, c3.shape, 1)
    step = 1
    while step < V7X_SUBLANES:
        c3 = c3 + jnp.where(sub >= step, pltpu.roll(c3, step, 1), 0.0)
        step *= 2
    ccarry = ccar[...]
    c_rows = []
    for g in range(n_grp):
        cg = c3[g] + ccarry
        c_rows.append(cg)
        ccarry = cg[V7X_SUBLANES - 1:V7X_SUBLANES, :]
    ccar[...] = ccarry
    c = jnp.concatenate(c_rows, axis=0)
    c_hi = c.astype(BF16).astype(F32)
    c_mid = (c - c_hi).astype(BF16).astype(F32)
    c_lo = (c - c_hi - c_mid).astype(BF16).astype(F32)
    lane = lax.broadcasted_iota(jnp.int32, c.shape, 1)
    piece = lane % N_SPLIT
    c_split = jnp.where(piece == 0, c_hi, jnp.where(piece == 1, c_mid, c_lo))
    n_bias = FOX_HEADS * N_SPLIT
    k_bias = jnp.where(lane < n_bias, -c_split,
                       jnp.where(lane < REF_LANE0 + N_SPLIT, 1.0, 0.0)).astype(BF16)
    c_split_t = pltpu.roll(c_split.T, n_bias, 0)
    row = lax.broadcasted_iota(jnp.int32, c_split_t.shape, 0)
    for hh in range(FOX_HEADS):
        sl = slice(hh * fox_dh, (hh + 1) * fox_dh)
        lo = hh * N_SPLIT
        ones_rows = (row >= lo) & (row < lo + N_SPLIT)
        c_rows_sel = (row >= n_bias + lo) & (row < n_bias + lo + N_SPLIT)
        q_bias = jnp.where(ones_rows, 1.0, jnp.where(c_rows_sel, c_split_t, 0.0)).astype(BF16)
        k_ref[0, hh, 0, :, 0:fox_dh] = kz[:, sl]
        k_ref[0, hh, 0, :, fox_dh:fox_dh + BIAS_LANES] = k_bias
        qt_ref[0, hh, 0:fox_dh, :] = qt[sl, :]
        qt_ref[0, hh, fox_dh:fox_dh + BIAS_LANES, :] = q_bias
        vt_ref[0, hh, 0] = vt[sl, :]

    mq = (_dot(u, wmq_ref[...]) * (mem_dh ** -0.5)).astype(BF16)
    yc = []
    for hh in range(MEM_HEADS):
        sl = slice(hh * mem_dh, (hh + 1) * mem_dh)
        lg = _dot_nt(mq[:, sl], kv_ref[0, :, sl])
        e = jnp.exp(lg - jnp.max(lg, axis=-1, keepdims=True))
        num = _dot(e.astype(BF16), kv_ref[0, :, d + hh * mem_dh:d + (hh + 1) * mem_dh])
        yc.append((num / jnp.sum(e, axis=-1, keepdims=True)).astype(BF16))
    pc = _dot(jnp.concatenate(yc, axis=1), wpc_ref[...])

    g0 = _sigmoid(_dot(u, wgate_ref[:, 0:d]) + bgate_ref[0:1, :])
    g2 = _sigmoid(_dot(u, wgate_ref[:, 2 * d:3 * d]) + bgate_ref[2:3, :])
    part_ref[0] = g0 * pa + g2 * pc
    g1_ref[0] = _sigmoid(_dot(u, wgate_ref[:, d:2 * d]) + bgate_ref[1:2, :])


def _mixer_in(h, kv, p):
    bsz, seq, d = h.shape
    tm = ROW_TILE
    n_t = seq // tm
    fox_dh = d // FOX_HEADS
    n_mem, n_kv = kv.shape[1], kv.shape[2]
    qk_dim = fox_dh + BIAS_LANES
    row_spec = pl.BlockSpec((1, tm, d), lambda b, j: (b, j, 0))
    in_specs = [
        row_spec,
        _const_spec((1, d)),
        _const_spec((d, d)), _const_spec((d, d)), _const_spec((2 * d, d)), _const_spec((d, d)),
        _const_spec((d, N_BRANCH * d)), _const_spec((d, BIAS_LANES)),
        _const_spec((CONV_W, d)), _const_spec((1, d)),
        _const_spec((LRU_BLOCKS, d // LRU_BLOCKS, 2 * d // LRU_BLOCKS)),
        _const_spec((1, d)), _const_spec((1, d)), _const_spec((1, d)), _const_spec((1, BIAS_LANES)),
        pl.BlockSpec((1, n_mem, n_kv), lambda b, j: (b, 0, 0)),
        _const_spec((N_BRANCH, d)), _const_spec((d, d)), _const_spec((d, d)),
    ]
    out_specs = [
        pl.BlockSpec((1, FOX_HEADS, qk_dim, tm), lambda b, j: (b, 0, 0, j)),
        pl.BlockSpec((1, FOX_HEADS, 1, tm, qk_dim), lambda b, j: (b, 0, j, 0, 0)),
        pl.BlockSpec((1, FOX_HEADS, 1, fox_dh, tm), lambda b, j: (b, 0, j, 0, 0)),
        row_spec, row_spec,
    ]
    out_shape = [
        jax.ShapeDtypeStruct((bsz, FOX_HEADS, qk_dim, seq), BF16),
        jax.ShapeDtypeStruct((bsz, FOX_HEADS, n_t, tm, qk_dim), BF16),
        jax.ShapeDtypeStruct((bsz, FOX_HEADS, n_t, fox_dh, tm), BF16),
        jax.ShapeDtypeStruct((bsz, seq, d), F32),
        jax.ShapeDtypeStruct((bsz, seq, d), F32),
    ]
    scratch = [
        pltpu.VMEM((CONV_W - 1, d), F32),
        pltpu.VMEM((tm, d), F32), pltpu.VMEM((tm, d), F32),
        pltpu.VMEM((1, d), F32), pltpu.VMEM((1, BIAS_LANES), F32),
    ]
    return pl.pallas_call(
        functools.partial(_mixer_in_kernel, q_scale=(fox_dh ** -0.5) * LOG2E),
        grid=(bsz, n_t),
        in_specs=in_specs, out_specs=out_specs, out_shape=out_shape, scratch_shapes=scratch,
        compiler_params=pltpu.CompilerParams(
            dimension_semantics=("arbitrary", "arbitrary"), vmem_limit_bytes=V7X_VMEM_LIMIT_BYTES),
        name="mixer_in",
    )(h, p["g_pre"], p["w_lru"], p["w_k"], p["w_qvt"], p["w_mq"], p["w_gate"], p["w_f"],
      p["conv_w"], p["conv_b"], p["w_ri"], p["b_r"], p["b_i"], p["lam"], p["b_f"], kv,
      p["b_gate"], p["w_pa"], p["w_pc"])


def _fox_kernel(qt_ref, k_ref, vt_ref, o_ref, q_eff, p0, p1, acc_scr, l_scr, m_scr, kmax_scr):
    tq = qt_ref.shape[3]
    n_c, tc = k_ref.shape[2], k_ref.shape[3]
    dh = o_ref.shape[2]
    per_q = tq // tc
    i = pl.program_id(2)
    p_buf = (p0, p1)
    ref_row0 = dh + REF_LANE0

    @pl.when(i == 0)
    def _():
        def body(c, mx):
            kk = k_ref[0, 0, c, :, 0:dh].astype(F32)
            return jnp.maximum(mx, jnp.sum(kk * kk, axis=1, keepdims=True))
        mx = lax.fori_loop(0, n_c, body, jnp.zeros((tc, 1), F32))
        kmax_scr[...] = jnp.max(mx, axis=0, keepdims=True)

    q = qt_ref[0, 0]
    qf = q[0:dh].astype(F32)
    ref = jnp.sqrt(jnp.sum(qf * qf, axis=0, keepdims=True) * kmax_scr[...]) * REF_SLACK
    r_hi = ref.astype(BF16).astype(F32)
    r_mid = (ref - r_hi).astype(BF16).astype(F32)
    r_lo = (ref - r_hi - r_mid).astype(BF16).astype(F32)
    row = lax.broadcasted_iota(jnp.int32, (V7X_BF16_SUBLANES, tq), 0)
    q_eff[...] = q
    q_eff[ref_row0:ref_row0 + V7X_BF16_SUBLANES, :] = jnp.where(
        row == 0, -r_hi, jnp.where(row == 1, -r_mid, jnp.where(row == 2, -r_lo, 0.0))).astype(BF16)
    acc_scr[...] = jnp.zeros_like(acc_scr)
    l_scr[...] = jnp.zeros_like(l_scr)

    def probs(tile, slot, half, masked):
        l_new = l_scr[...]
        for sub in range(per_q):
            row0 = (half * per_q + sub) * tc
            if masked:
                s = _dot(k_ref[0, 0, tile * per_q + sub], q_eff[:, sub * tc:tq])
                key = lax.broadcasted_iota(jnp.int32, s.shape, 0)
                qry = lax.broadcasted_iota(jnp.int32, s.shape, 1)
                p = jnp.exp2(jnp.where(key <= qry, s, NEG_BIG))
                if sub > 0:
                    p = jnp.concatenate([jnp.zeros((tc, sub * tc), F32), p], axis=1)
            else:
                p = jnp.exp2(_dot(k_ref[0, 0, tile * per_q + sub], q_eff[...]))
            l_new = l_new + jnp.sum(p, axis=0, keepdims=True)
            p_buf[slot][row0:row0 + tc, :] = p.astype(BF16)
        l_scr[...] = l_new

    def accumulate(t, slot):
        tiles = (jnp.where(t == 0, i, first_full + 2 * (t - 1)),
                 jnp.where(t == 0, 0, first_full + 2 * (t - 1) + 1))
        upd = None
        for half in range(JOB_TILES):
            for sub in range(per_q):
                row0 = (half * per_q + sub) * tc
                term = _dot(vt_ref[0, 0, tiles[half] * per_q + sub], p_buf[slot][row0:row0 + tc, :])
                upd = term if upd is None else upd + term
        acc_scr[...] += upd

    def by_parity(x, fn):
        for par in range(2):
            pl.when(x % 2 == par)(functools.partial(fn, par))

    alone = (i + 1) % 2
    first_full = 1 - alone
    n_jobs = (i + 2) // 2
    probs(i, 0, 0, True)

    @pl.when(alone == 0)
    def _():
        probs(0, 0, 1, False)

    @pl.when(alone == 1)
    def _():
        p0[tq:2 * tq, :] = jnp.zeros((tq, tq), BF16)

    def step(t, carry):
        def one(par):
            for half in range(JOB_TILES):
                probs(first_full + 2 * t + half, 1 - par, half, False)
            accumulate(t, par)
        by_parity(t, one)
        return carry

    lax.fori_loop(0, n_jobs - 1, step, 0)
    by_parity(n_jobs - 1, lambda par: accumulate(n_jobs - 1, par))

    @pl.when(jnp.logical_not(jnp.min(l_scr[...]) >= L_UNDERFLOW))
    def _():
        m_scr[...] = jnp.full_like(m_scr, NEG_BIG)
        l_scr[...] = jnp.zeros_like(l_scr)
        acc_scr[...] = jnp.zeros_like(acc_scr)

        def chunk(c, carry):
            s = _dot(k_ref[0, 0, c], qt_ref[0, 0])
            key = lax.broadcasted_iota(jnp.int32, s.shape, 0) + c * tc
            qry = lax.broadcasted_iota(jnp.int32, s.shape, 1) + i * tq
            s = jnp.where(key <= qry, s, NEG_BIG)
            m_old = m_scr[...]
            m_new = jnp.maximum(m_old, jnp.max(s, axis=0, keepdims=True))
            p = jnp.exp2(s - m_new)
            alpha = jnp.exp2(m_old - m_new)
            l_scr[...] = alpha * l_scr[...] + jnp.sum(p, axis=0, keepdims=True)
            acc_scr[...] = alpha * acc_scr[...] + _dot(vt_ref[0, 0, c], p.astype(BF16))
            m_scr[...] = m_new
            return carry

        lax.fori_loop(0, (i + 1) * per_q, chunk, 0)

    o_ref[0] = (acc_scr[...] / l_scr[...]).T.astype(o_ref.dtype)


def _fox_attention(qt, k, vt, d_model):
    bsz, n_heads, qk_dim, seq = qt.shape
    n_c, tc = k.shape[2], k.shape[3]
    dh = vt.shape[3]
    tq = Q_TILE
    resident = lambda shape: pl.BlockSpec(shape, lambda b, h, i: (b, h, 0, 0, 0))
    return pl.pallas_call(
        _fox_kernel,
        grid=(bsz, n_heads, seq // tq),
        in_specs=[
            pl.BlockSpec((1, 1, qk_dim, tq), lambda b, h, i: (b, h, 0, i)),
            resident((1, 1, n_c, tc, qk_dim)),
            resident((1, 1, n_c, dh, tc)),
        ],
        out_specs=pl.BlockSpec((1, tq, dh), lambda b, h, i: (b, i, h)),
        out_shape=jax.ShapeDtypeStruct((bsz, seq, d_model), BF16),
        scratch_shapes=[
            pltpu.VMEM((qk_dim, tq), BF16),
            pltpu.VMEM((JOB_TILES * tq, tq), BF16), pltpu.VMEM((JOB_TILES * tq, tq), BF16),
            pltpu.VMEM((dh, tq), F32), pltpu.VMEM((1, tq), F32),
            pltpu.VMEM((1, tq), F32),
            pltpu.VMEM((1, 1), F32)],
        compiler_params=pltpu.CompilerParams(
            dimension_semantics=("arbitrary", "arbitrary", "arbitrary"),
            vmem_limit_bytes=V7X_VMEM_LIMIT_BYTES),
        name="fox_attention",
    )(qt, k, vt)


def _post_kernel(yb_ref, part_ref, g1_ref, h_ref, wpb_ref, wo_ref, gpost_ref, gmlp_ref, wup_ref,
                 wdown_ref, gmlp_post_ref, out_ref):
    merged = part_ref[0] + g1_ref[0] * _dot(yb_ref[0], wpb_ref[...])
    h1 = h_ref[0] + _rms(_dot(merged.astype(BF16), wo_ref[...]), gpost_ref[...])
    u = _rms(h1, gmlp_ref[...]).astype(BF16)
    act = jnp.square(jnp.maximum(_dot(u, wup_ref[...]), 0.0))
    out_ref[0] = h1 + _rms(_dot(act.astype(BF16), wdown_ref[...]), gmlp_post_ref[...])


def _post(yb, part, g1, h, p):
    bsz, seq, d = h.shape
    d_ff = p["w_up"].shape[1]
    tm = ROW_TILE
    row_spec = pl.BlockSpec((1, tm, d), lambda b, j: (b, j, 0))
    return pl.pallas_call(
        _post_kernel,
        grid=(bsz, seq // tm),
        in_specs=[row_spec, row_spec, row_spec, row_spec,
                  _const_spec((d, d)), _const_spec((d, d)), _const_spec((1, d)), _const_spec((1, d)),
                  _const_spec((d, d_ff)), _const_spec((d_ff, d)), _const_spec((1, d))],
        out_specs=row_spec,
        out_shape=jax.ShapeDtypeStruct((bsz, seq, d), F32),
        compiler_params=pltpu.CompilerParams(
            dimension_semantics=("arbitrary", "arbitrary"), vmem_limit_bytes=V7X_VMEM_LIMIT_BYTES),
        name="post",
    )(yb, part, g1, h, p["w_pb"], p["w_o"], p["g_post"], p["g_mlp_pre"], p["w_up"], p["w_down"],
      p["g_mlp_post"])


def _layer_params(l, d, g_mix_pre, w_in, conv_w, conv_b, w_r, b_r, w_i, b_i, lru_lambda, b_f,
                  b_gate, w_pa, w_pb, w_pc, w_o, g_mix_post, g_mlp_pre, w_up, w_down, g_mlp_post):
    wi = w_in[l]
    off_q, off_k, off_v, off_mq, off_gate = d, 2 * d, 3 * d, 4 * d, 5 * d
    off_f = off_gate + N_BRANCH * d
    w_f = jnp.repeat(wi[:, off_f:off_f + FOX_HEADS], N_SPLIT, axis=1)
    pad = BIAS_LANES - FOX_HEADS * N_SPLIT
    row = lambda v: v.reshape(1, -1).astype(F32)
    return dict(
        g_pre=row(g_mix_pre[l]),
        w_lru=wi[:, 0:d].astype(BF16),
        w_k=wi[:, off_k:off_k + d].astype(BF16),
        w_qvt=jnp.concatenate([wi[:, off_q:off_q + d].T, wi[:, off_v:off_v + d].T], axis=0).astype(BF16),
        w_mq=wi[:, off_mq:off_mq + d].astype(BF16),
        w_gate=wi[:, off_gate:off_gate + N_BRANCH * d].astype(BF16),
        w_f=jnp.pad(w_f, ((0, 0), (0, pad))).astype(BF16),
        conv_w=conv_w[l].astype(F32), conv_b=row(conv_b[l]),
        w_ri=jnp.concatenate([w_r[l], w_i[l]], axis=-1).astype(BF16),
        b_r=row(b_r[l]), b_i=row(b_i[l]), lam=row(lru_lambda[l]),
        b_f=jnp.pad(jnp.repeat(b_f[l], N_SPLIT), (0, pad)).reshape(1, -1).astype(F32),
        b_gate=b_gate[l].astype(F32),
        w_pa=w_pa[l].astype(BF16), w_pb=w_pb[l].astype(BF16), w_pc=w_pc[l].astype(BF16),
        w_o=w_o[l].astype(BF16), g_post=row(g_mix_post[l]), g_mlp_pre=row(g_mlp_pre[l]),
        w_up=w_up[l].astype(BF16), w_down=w_down[l].astype(BF16), g_mlp_post=row(g_mlp_post[l]),
    )


def kernel(x, mem, g_mix_pre, w_in, conv_w, conv_b, w_r, b_r, w_i, b_i, lru_lambda, b_f, g_mem, w_mem_kv, b_gate, w_pa, w_pb, w_pc, w_o, g_mix_post, g_mlp_pre, w_up, w_down, g_mlp_post):
    bsz, seq, d = x.shape
    depth = w_in.shape[0]
    assert seq % Q_TILE == 0 and Q_TILE % ROW_TILE == 0 and d % (FOX_HEADS * V7X_LANES) == 0
    kv_all = _mem_kv(mem, g_mem, w_mem_kv)
    h = x
    for l in range(depth):
        p = _layer_params(l, d, g_mix_pre, w_in, conv_w, conv_b, w_r, b_r, w_i, b_i, lru_lambda,
                          b_f, b_gate, w_pa, w_pb, w_pc, w_o, g_mix_post, g_mlp_pre, w_up, w_down,
                          g_mlp_post)
        qt, k, vt, part, g1 = _mixer_in(h, kv_all[l], p)
        yb = _fox_attention(qt, k, vt, d)
        h = _post(yb, part, g1, h, p)
    return h
```

```python
import functools

import jax
import jax.numpy as jnp
from jax import lax
from jax.experimental import pallas as pl
from jax.experimental.pallas import tpu as pltpu

EPS = 1e-6
LRU_BLOCKS = 8
LRU_C = 8.0
CONV_W = 4
FOX_HEADS = 8
MEM_HEADS = 4
N_BRANCH = 3
LOG2E = 1.4426950408889634
NEG_BIG = -1e30

V7X_LANES = 128
V7X_SUBLANES = 8
V7X_BF16_SUBLANES = 16
V7X_VMEM_LIMIT_BYTES = 56 * 1024 * 1024

ROW_TILE = 512
POST_ROW_TILE = 512
Q_TILE = 2048
JOB_TILES = 1
BIAS_LANES = V7X_LANES
N_SPLIT = 3
REF_LANE0 = 2 * FOX_HEADS * N_SPLIT
REF_SLACK = 1.01
L_UNDERFLOW = 2.0 ** -60
assert REF_LANE0 % V7X_BF16_SUBLANES == 0 and REF_LANE0 + N_SPLIT <= BIAS_LANES

F32 = jnp.float32
BF16 = jnp.bfloat16


def _const_spec(shape):
    nd = len(shape)
    return pl.BlockSpec(shape, lambda *_: (0,) * nd, pipeline_mode=pl.Buffered(1))


def _rms(x, g):
    return x * lax.rsqrt(jnp.mean(x * x, axis=-1, keepdims=True) + EPS) * g


def _sigmoid(x):
    return 1.0 / (1.0 + jnp.exp2(x * (-LOG2E)))


def _softplus(x):
    return jnp.maximum(x, 0.0) + jnp.log1p(jnp.exp(-jnp.abs(x)))


def _dot(a, b):
    return jnp.dot(a, b, preferred_element_type=F32)


def _dot_nt(a, b):
    return lax.dot_general(a, b, (((1,), (1,)), ((), ())), preferred_element_type=F32)


def _mem_kv_kernel(mem_ref, g_ref, w_ref, kv_ref):
    u = _rms(mem_ref[0], g_ref[0]).astype(BF16)
    kv_ref[0, 0] = _dot(u, w_ref[0]).astype(BF16)


def _mem_kv(mem, g_mem, w_mem_kv):
    depth, d_model, n_kv = w_mem_kv.shape
    bsz, n_mem, _ = mem.shape
    return pl.pallas_call(
        _mem_kv_kernel,
        grid=(depth, bsz),
        in_specs=[
            pl.BlockSpec((1, n_mem, d_model), lambda l, b: (b, 0, 0)),
            pl.BlockSpec((1, 1, d_model), lambda l, b: (l, 0, 0)),
            pl.BlockSpec((1, d_model, n_kv), lambda l, b: (l, 0, 0)),
        ],
        out_specs=pl.BlockSpec((1, 1, n_mem, n_kv), lambda l, b: (l, b, 0, 0)),
        out_shape=jax.ShapeDtypeStruct((depth, bsz, n_mem, n_kv), BF16),
        compiler_params=pltpu.CompilerParams(
            dimension_semantics=("arbitrary", "arbitrary"), vmem_limit_bytes=V7X_VMEM_LIMIT_BYTES),
        name="mem_kv",
    )(mem, g_mem.reshape(depth, 1, d_model), w_mem_kv.astype(BF16))


def _mixer_in_kernel(h_ref, gpre_ref, wlru_ref, wk_ref, wqvt_ref, wmq_ref, wgate_ref, wf_ref,
                     convw_ref, convb_ref, wri_ref, br_ref, bi_ref, lam_ref, bf_ref, kv_ref,
                     bgate_ref, wpa_ref, wpc_ref,
                     qt_ref, k_ref, vt_ref, part_ref, g1_ref,
                     conv_car, a_scr, b_scr, hcar, ccar, *, q_scale):
    tm, d = h_ref.shape[1], h_ref.shape[2]
    n_grp = tm // V7X_SUBLANES
    lru_blk = d // LRU_BLOCKS
    fox_dh = d // FOX_HEADS
    mem_dh = d // MEM_HEADS

    @pl.when(pl.program_id(1) == 0)
    def _():
        conv_car[...] = jnp.zeros_like(conv_car)
        hcar[...] = jnp.zeros_like(hcar)
        ccar[...] = jnp.zeros_like(ccar)

    u = _rms(h_ref[0], gpre_ref[...]).astype(BF16)

    z = _dot(u, wlru_ref[...])
    first_row = lax.broadcasted_iota(jnp.int32, z.shape, 0) == 0
    y = convw_ref[0:1, :] * z
    for kk in range(1, CONV_W):
        rot = pltpu.roll(y, 1, 0)
        y = jnp.where(first_row, conv_car[kk - 1:kk, :], rot) + convw_ref[kk:kk + 1, :] * z
        conv_car[kk - 1:kk, :] = rot[0:1, :]
    xc = y + convb_ref[...]
    xcb = xc.astype(BF16)
    neg_c_sp = (-LRU_C * LOG2E) * _softplus(-lam_ref[...])
    for hb in range(LRU_BLOCKS):
        sl = slice(hb * lru_blk, (hb + 1) * lru_blk)
        ri = _dot(xcb[:, sl], wri_ref[hb])
        r = _sigmoid(ri[:, :lru_blk] + br_ref[:, sl])
        i = _sigmoid(ri[:, lru_blk:] + bi_ref[:, sl])
        a = jnp.exp2(r * neg_c_sp[:, sl])
        a_scr[:, sl] = a
        b_scr[:, sl] = jnp.exp2(0.5 * jnp.log2(1.0 - a * a)) * i * xc[:, sl]
    a3 = a_scr[...].reshape(n_grp, V7X_SUBLANES, d)
    b3 = b_scr[...].reshape(n_grp, V7X_SUBLANES, d)
    sub = lax.broadcasted_iota(jnp.int32, a3.shape, 1)
    step = 1
    while step < V7X_SUBLANES:
        keep = sub >= step
        a_prev = jnp.where(keep, pltpu.roll(a3, step, 1), 1.0)
        b_prev = jnp.where(keep, pltpu.roll(b3, step, 1), 0.0)
        b3 = a3 * b_prev + b3
        a3 = a3 * a_prev
        step *= 2
    carry = hcar[...]
    for g in range(n_grp):
        hg = a3[g] * carry + b3[g]
        b_scr[g * V7X_SUBLANES:(g + 1) * V7X_SUBLANES, :] = hg
        carry = hg[V7X_SUBLANES - 1:V7X_SUBLANES, :]
    hcar[...] = carry
    pa = _dot(b_scr[...].astype(BF16), wpa_ref[...])

    kz = _dot(u, wk_ref[...]).astype(BF16)
    zt = _dot_nt(wqvt_ref[...], u)
    qt = (zt[:d] * q_scale).astype(BF16)
    vt = zt[d:].astype(BF16)
    zf = _dot(u, wf_ref[...]) + bf_ref[...]
    lf = (jnp.minimum(zf, 0.0) - jnp.log1p(jnp.exp(-jnp.abs(zf)))) * LOG2E
    c3 = lf.reshape(n_grp, V7X_SUBLANES, BIAS_LANES)
    sub = lax.broadcasted_iota(jnp.int32, c3.shape, 1)
    step = 1
    while step < V7X_SUBLANES:
        c3 = c3 + jnp.where(sub >= step, pltpu.roll(c3, step, 1), 0.0)
        step *= 2
    ccarry = ccar[...]
    c_rows = []
    for g in range(n_grp):
        cg = c3[g] + ccarry
        c_rows.append(cg)
        ccarry = cg[V7X_SUBLANES - 1:V7X_SUBLANES, :]
    ccar[...] = ccarry
    c = jnp.concatenate(c_rows, axis=0)
    c_hi = c.astype(BF16).astype(F32)
    c_mid = (c - c_hi).astype(BF16).astype(F32)
    c_lo = (c - c_hi - c_mid).astype(BF16).astype(F32)
    lane = lax.broadcasted_iota(jnp.int32, c.shape, 1)
    piece = lane % N_SPLIT
    c_split = jnp.where(piece == 0, c_hi, jnp.where(piece == 1, c_mid, c_lo))
    n_bias = FOX_HEADS * N_SPLIT
    k_bias = jnp.where(lane < n_bias, -c_split,
                       jnp.where(lane < REF_LANE0 + N_SPLIT, 1.0, 0.0)).astype(BF16)
    c_split_t = pltpu.roll(c_split.T, n_bias, 0)
    row = lax.broadcasted_iota(jnp.int32, c_split_t.shape, 0)
    for hh in range(FOX_HEADS):
        sl = slice(hh * fox_dh, (hh + 1) * fox_dh)
        lo = hh * N_SPLIT
        ones_rows = (row >= lo) & (row < lo + N_SPLIT)
        c_rows_sel = (row >= n_bias + lo) & (row < n_bias + lo + N_SPLIT)
        q_bias = jnp.where(ones_rows, 1.0, jnp.where(c_rows_sel, c_split_t, 0.0)).astype(BF16)
        k_ref[0, hh, 0, :, 0:fox_dh] = kz[:, sl]
        k_ref[0, hh, 0, :, fox_dh:fox_dh + BIAS_LANES] = k_bias
        qt_ref[0, hh, 0:fox_dh, :] = qt[sl, :]
        qt_ref[0, hh, fox_dh:fox_dh + BIAS_LANES, :] = q_bias
        vt_ref[0, hh, 0] = vt[sl, :]

    mq = (_dot(u, wmq_ref[...]) * (mem_dh ** -0.5)).astype(BF16)
    yc = []
    for hh in range(MEM_HEADS):
        sl = slice(hh * mem_dh, (hh + 1) * mem_dh)
        lg = _dot_nt(mq[:, sl], kv_ref[0, :, sl])
        e = jnp.exp(lg - jnp.max(lg, axis=-1, keepdims=True))
        num = _dot(e.astype(BF16), kv_ref[0, :, d + hh * mem_dh:d + (hh + 1) * mem_dh])
        yc.append((num / jnp.sum(e, axis=-1, keepdims=True)).astype(BF16))
    pc = _dot(jnp.concatenate(yc, axis=1), wpc_ref[...])

    g0 = _sigmoid(_dot(u, wgate_ref[:, 0:d]) + bgate_ref[0:1, :])
    g2 = _sigmoid(_dot(u, wgate_ref[:, 2 * d:3 * d]) + bgate_ref[2:3, :])
    part_ref[0] = g0 * pa + g2 * pc
    g1_ref[0] = _sigmoid(_dot(u, wgate_ref[:, d:2 * d]) + bgate_ref[1:2, :])


def _mixer_in(h, kv, p):
    bsz, seq, d = h.shape
    tm = ROW_TILE
    n_t = seq // tm
    fox_dh = d // FOX_HEADS
    n_mem, n_kv = kv.shape[1], kv.shape[2]
    qk_dim = fox_dh + BIAS_LANES
    row_spec = pl.BlockSpec((1, tm, d), lambda b, j: (b, j, 0))
    in_specs = [
        row_spec,
        _const_spec((1, d)),
        _const_spec((d, d)), _const_spec((d, d)), _const_spec((2 * d, d)), _const_spec((d, d)),
        _const_spec((d, N_BRANCH * d)), _const_spec((d, BIAS_LANES)),
        _const_spec((CONV_W, d)), _const_spec((1, d)),
        _const_spec((LRU_BLOCKS, d // LRU_BLOCKS, 2 * d // LRU_BLOCKS)),
        _const_spec((1, d)), _const_spec((1, d)), _const_spec((1, d)), _const_spec((1, BIAS_LANES)),
        pl.BlockSpec((1, n_mem, n_kv), lambda b, j: (b, 0, 0)),
        _const_spec((N_BRANCH, d)), _const_spec((d, d)), _const_spec((d, d)),
    ]
    out_specs = [
        pl.BlockSpec((1, FOX_HEADS, qk_dim, tm), lambda b, j: (b, 0, 0, j)),
        pl.BlockSpec((1, FOX_HEADS, 1, tm, qk_dim), lambda b, j: (b, 0, j, 0, 0)),
        pl.BlockSpec((1, FOX_HEADS, 1, fox_dh, tm), lambda b, j: (b, 0, j, 0, 0)),
        row_spec, row_spec,
    ]
    out_shape = [
        jax.ShapeDtypeStruct((bsz, FOX_HEADS, qk_dim, seq), BF16),
        jax.ShapeDtypeStruct((bsz, FOX_HEADS, n_t, tm, qk_dim), BF16),
        jax.ShapeDtypeStruct((bsz, FOX_HEADS, n_t, fox_dh, tm), BF16),
        jax.ShapeDtypeStruct((bsz, seq, d), F32),
        jax.ShapeDtypeStruct((bsz, seq, d), F32),
    ]
    scratch = [
        pltpu.VMEM((CONV_W - 1, d), F32),
        pltpu.VMEM((tm, d), F32), pltpu.VMEM((tm, d), F32),
        pltpu.VMEM((1, d), F32), pltpu.VMEM((1, BIAS_LANES), F32),
    ]
    return pl.pallas_call(
        functools.partial(_mixer_in_kernel, q_scale=(fox_dh ** -0.5) * LOG2E),
        grid=(bsz, n_t),
        in_specs=in_specs, out_specs=out_specs, out_shape=out_shape, scratch_shapes=scratch,
        compiler_params=pltpu.CompilerParams(
            dimension_semantics=("arbitrary", "arbitrary"), vmem_limit_bytes=V7X_VMEM_LIMIT_BYTES),
        name="mixer_in",
    )(h, p["g_pre"], p["w_lru"], p["w_k"], p["w_qvt"], p["w_mq"], p["w_gate"], p["w_f"],
      p["conv_w"], p["conv_b"], p["w_ri"], p["b_r"], p["b_i"], p["lam"], p["b_f"], kv,
      p["b_gate"], p["w_pa"], p["w_pc"])


def _fox_kernel(qt_ref, k_ref, vt_ref, o_ref, q_eff, p0, p1, acc_scr, l_scr, m_scr, kmax_scr):
    tq = qt_ref.shape[3]
    n_c, tc = k_ref.shape[2], k_ref.shape[3]
    dh = o_ref.shape[2]
    per_q = tq // tc
    i = pl.program_id(2)
    p_buf = (p0, p1)
    ref_row0 = dh + REF_LANE0

    @pl.when(i == 0)
    def _():
        def body(c, mx):
            kk = k_ref[0, 0, c, :, 0:dh].astype(F32)
            return jnp.maximum(mx, jnp.sum(kk * kk, axis=1, keepdims=True))
        mx = lax.fori_loop(0, n_c, body, jnp.zeros((tc, 1), F32))
        kmax_scr[...] = jnp.max(mx, axis=0, keepdims=True)

    q = qt_ref[0, 0]
    qf = q[0:dh].astype(F32)
    ref = jnp.sqrt(jnp.sum(qf * qf, axis=0, keepdims=True) * kmax_scr[...]) * REF_SLACK
    r_hi = ref.astype(BF16).astype(F32)
    r_mid = (ref - r_hi).astype(BF16).astype(F32)
    r_lo = (ref - r_hi - r_mid).astype(BF16).astype(F32)
    row = lax.broadcasted_iota(jnp.int32, (V7X_BF16_SUBLANES, tq), 0)
    q_eff[...] = q
    q_eff[ref_row0:ref_row0 + V7X_BF16_SUBLANES, :] = jnp.where(
        row == 0, -r_hi, jnp.where(row == 1, -r_mid, jnp.where(row == 2, -r_lo, 0.0))).astype(BF16)
    acc_scr[...] = jnp.zeros_like(acc_scr)
    l_scr[...] = jnp.zeros_like(l_scr)

    def probs(tile, slot, half, masked):
        l_new = l_scr[...]
        for sub in range(per_q):
            row0 = (half * per_q + sub) * tc
            if masked:
                s = _dot(k_ref[0, 0, tile * per_q + sub], q_eff[:, sub * tc:tq])
                key = lax.broadcasted_iota(jnp.int32, s.shape, 0)
                qry = lax.broadcasted_iota(jnp.int32, s.shape, 1)
                p = jnp.exp2(jnp.where(key <= qry, s, NEG_BIG))
                if sub > 0:
                    p = jnp.concatenate([jnp.zeros((tc, sub * tc), F32), p], axis=1)
            else:
                p = jnp.exp2(_dot(k_ref[0, 0, tile * per_q + sub], q_eff[...]))
            l_new = l_new + jnp.sum(p, axis=0, keepdims=True)
            p_buf[slot][row0:row0 + tc, :] = p.astype(BF16)
        l_scr[...] = l_new

    def accumulate(t, slot):
        upd = None
        for half in range(JOB_TILES):
            tile = jnp.where(t == 0, i if half == 0 else half - 1, first_full + JOB_TILES * (t - 1) + half)
            for sub in range(per_q):
                row0 = (half * per_q + sub) * tc
                term = _dot(vt_ref[0, 0, tile * per_q + sub], p_buf[slot][row0:row0 + tc, :])
                upd = term if upd is None else upd + term
        acc_scr[...] += upd

    def by_parity(x, fn):
        for par in range(2):
            pl.when(x % 2 == par)(functools.partial(fn, par))

    n_pad = (JOB_TILES - (i + 1) % JOB_TILES) % JOB_TILES
    first_full = JOB_TILES - 1 - n_pad
    n_jobs = (i + 1 + n_pad) // JOB_TILES
    probs(i, 0, 0, True)
    for pad in range(JOB_TILES if JOB_TILES > 1 else 0):
        @pl.when(n_pad == pad)
        def _():
            for half in range(1, JOB_TILES):
                if half <= JOB_TILES - 1 - pad:
                    probs(half - 1, 0, half, False)
                else:
                    p0[half * tq:(half + 1) * tq, :] = jnp.zeros((tq, tq), BF16)

    def step(t, carry):
        def one(par):
            for half in range(JOB_TILES):
                probs(first_full + JOB_TILES * t + half, 1 - par, half, False)
            accumulate(t, par)
        by_parity(t, one)
        return carry

    lax.fori_loop(0, n_jobs - 1, step, 0)
    by_parity(n_jobs - 1, lambda par: accumulate(n_jobs - 1, par))

    @pl.when(jnp.logical_not(jnp.min(l_scr[...]) >= L_UNDERFLOW))
    def _():
        m_scr[...] = jnp.full_like(m_scr, NEG_BIG)
        l_scr[...] = jnp.zeros_like(l_scr)
        acc_scr[...] = jnp.zeros_like(acc_scr)

        def chunk(c, carry):
            s = _dot(k_ref[0, 0, c], qt_ref[0, 0])
            key = lax.broadcasted_iota(jnp.int32, s.shape, 0) + c * tc
            qry = lax.broadcasted_iota(jnp.int32, s.shape, 1) + i * tq
            s = jnp.where(key <= qry, s, NEG_BIG)
            m_old = m_scr[...]
            m_new = jnp.maximum(m_old, jnp.max(s, axis=0, keepdims=True))
            p = jnp.exp2(s - m_new)
            alpha = jnp.exp2(m_old - m_new)
            l_scr[...] = alpha * l_scr[...] + jnp.sum(p, axis=0, keepdims=True)
            acc_scr[...] = alpha * acc_scr[...] + _dot(vt_ref[0, 0, c], p.astype(BF16))
            m_scr[...] = m_new
            return carry

        lax.fori_loop(0, (i + 1) * per_q, chunk, 0)

    o_ref[0] = (acc_scr[...] / l_scr[...]).T.astype(o_ref.dtype)


def _fox_attention(qt, k, vt, d_model):
    bsz, n_heads, qk_dim, seq = qt.shape
    n_c, tc = k.shape[2], k.shape[3]
    dh = vt.shape[3]
    tq = Q_TILE
    resident = lambda shape: pl.BlockSpec(shape, lambda b, h, i: (b, h, 0, 0, 0))
    return pl.pallas_call(
        _fox_kernel,
        grid=(bsz, n_heads, seq // tq),
        in_specs=[
            pl.BlockSpec((1, 1, qk_dim, tq), lambda b, h, i: (b, h, 0, i)),
            resident((1, 1, n_c, tc, qk_dim)),
            resident((1, 1, n_c, dh, tc)),
        ],
        out_specs=pl.BlockSpec((1, tq, dh), lambda b, h, i: (b, i, h)),
        out_shape=jax.ShapeDtypeStruct((bsz, seq, d_model), BF16),
        scratch_shapes=[
            pltpu.VMEM((qk_dim, tq), BF16),
            pltpu.VMEM((JOB_TILES * tq, tq), BF16), pltpu.VMEM((JOB_TILES * tq, tq), BF16),
            pltpu.VMEM((dh, tq), F32), pltpu.VMEM((1, tq), F32),
            pltpu.VMEM((1, tq), F32),
            pltpu.VMEM((1, 1), F32)],
        compiler_params=pltpu.CompilerParams(
            dimension_semantics=("arbitrary", "arbitrary", "arbitrary"),
            vmem_limit_bytes=V7X_VMEM_LIMIT_BYTES),
        name="fox_attention",
    )(qt, k, vt)


def _post_kernel(yb_ref, part_ref, g1_ref, h_ref, wpb_ref, wo_ref, gpost_ref, gmlp_ref, wup_ref,
                 wdown_ref, gmlp_post_ref, out_ref):
    merged = part_ref[0] + g1_ref[0] * _dot(yb_ref[0], wpb_ref[...])
    h1 = h_ref[0] + _rms(_dot(merged.astype(BF16), wo_ref[...]), gpost_ref[...])
    u = _rms(h1, gmlp_ref[...]).astype(BF16)
    act = jnp.square(jnp.maximum(_dot(u, wup_ref[...]), 0.0))
    out_ref[0] = h1 + _rms(_dot(act.astype(BF16), wdown_ref[...]), gmlp_post_ref[...])


def _post(yb, part, g1, h, p):
    bsz, seq, d = h.shape
    d_ff = p["w_up"].shape[1]
    tm = POST_ROW_TILE
    row_spec = pl.BlockSpec((1, tm, d), lambda b, j: (b, j, 0))
    return pl.pallas_call(
        _post_kernel,
        grid=(bsz, seq // tm),
        in_specs=[row_spec, row_spec, row_spec, row_spec,
                  _const_spec((d, d)), _const_spec((d, d)), _const_spec((1, d)), _const_spec((1, d)),
                  _const_spec((d, d_ff)), _const_spec((d_ff, d)), _const_spec((1, d))],
        out_specs=row_spec,
        out_shape=jax.ShapeDtypeStruct((bsz, seq, d), F32),
        compiler_params=pltpu.CompilerParams(
            dimension_semantics=("arbitrary", "arbitrary"), vmem_limit_bytes=V7X_VMEM_LIMIT_BYTES),
        name="post",
    )(yb, part, g1, h, p["w_pb"], p["w_o"], p["g_post"], p["g_mlp_pre"], p["w_up"], p["w_down"],
      p["g_mlp_post"])


def _layer_params(l, d, g_mix_pre, w_in, conv_w, conv_b, w_r, b_r, w_i, b_i, lru_lambda, b_f,
                  b_gate, w_pa, w_pb, w_pc, w_o, g_mix_post, g_mlp_pre, w_up, w_down, g_mlp_post):
    wi = w_in[l]
    off_q, off_k, off_v, off_mq, off_gate = d, 2 * d, 3 * d, 4 * d, 5 * d
    off_f = off_gate + N_BRANCH * d
    w_f = jnp.repeat(wi[:, off_f:off_f + FOX_HEADS], N_SPLIT, axis=1)
    pad = BIAS_LANES - FOX_HEADS * N_SPLIT
    row = lambda v: v.reshape(1, -1).astype(F32)
    return dict(
        g_pre=row(g_mix_pre[l]),
        w_lru=wi[:, 0:d].astype(BF16),
        w_k=wi[:, off_k:off_k + d].astype(BF16),
        w_qvt=jnp.concatenate([wi[:, off_q:off_q + d].T, wi[:, off_v:off_v + d].T], axis=0).astype(BF16),
        w_mq=wi[:, off_mq:off_mq + d].astype(BF16),
        w_gate=wi[:, off_gate:off_gate + N_BRANCH * d].astype(BF16),
        w_f=jnp.pad(w_f, ((0, 0), (0, pad))).astype(BF16),
        conv_w=conv_w[l].astype(F32), conv_b=row(conv_b[l]),
        w_ri=jnp.concatenate([w_r[l], w_i[l]], axis=-1).astype(BF16),
        b_r=row(b_r[l]), b_i=row(b_i[l]), lam=row(lru_lambda[l]),
        b_f=jnp.pad(jnp.repeat(b_f[l], N_SPLIT), (0, pad)).reshape(1, -1).astype(F32),
        b_gate=b_gate[l].astype(F32),
        w_pa=w_pa[l].astype(BF16), w_pb=w_pb[l].astype(BF16), w_pc=w_pc[l].astype(BF16),
        w_o=w_o[l].astype(BF16), g_post=row(g_mix_post[l]), g_mlp_pre=row(g_mlp_pre[l]),
        w_up=w_up[l].astype(BF16), w_down=w_down[l].astype(BF16), g_mlp_post=row(g_mlp_post[l]),
    )


def kernel(x, mem, g_mix_pre, w_in, conv_w, conv_b, w_r, b_r, w_i, b_i, lru_lambda, b_f, g_mem, w_mem_kv, b_gate, w_pa, w_pb, w_pc, w_o, g_mix_post, g_mlp_pre, w_up, w_down, g_mlp_post):
    bsz, seq, d = x.shape
    depth = w_in.shape[0]
    assert seq % Q_TILE == 0 and Q_TILE % ROW_TILE == 0 and d % (FOX_HEADS * V7X_LANES) == 0
    kv_all = _mem_kv(mem, g_mem, w_mem_kv)
    h = x
    for l in range(depth):
        p = _layer_params(l, d, g_mix_pre, w_in, conv_w, conv_b, w_r, b_r, w_i, b_i, lru_lambda,
                          b_f, b_gate, w_pa, w_pb, w_pc, w_o, g_mix_post, g_mlp_pre, w_up, w_down,
                          g_mlp_post)
        qt, k, vt, part, g1 = _mixer_in(h, kv_all[l], p)
        yb = _fox_attention(qt, k, vt, d)
        h = _post(yb, part, g1, h, p)
    return h
```

```python
import functools

import jax
import jax.numpy as jnp
from jax import lax
from jax.experimental import pallas as pl
from jax.experimental.pallas import tpu as pltpu

EPS = 1e-6
LRU_BLOCKS = 8
LRU_C = 8.0
CONV_W = 4
FOX_HEADS = 8
MEM_HEADS = 4
N_BRANCH = 3
LOG2E = 1.4426950408889634
NEG_BIG = -1e30

V7X_LANES = 128
V7X_SUBLANES = 8
V7X_BF16_SUBLANES = 16
V7X_VMEM_LIMIT_BYTES = 56 * 1024 * 1024

ROW_TILE = 512
POST_ROW_TILE = 512
Q_TILE = 2048
BIAS_LANES = V7X_LANES
N_SPLIT = 3
REF_LANE0 = 2 * FOX_HEADS * N_SPLIT
REF_SLACK = 1.01
L_UNDERFLOW = 2.0 ** -60
assert REF_LANE0 % V7X_BF16_SUBLANES == 0 and REF_LANE0 + N_SPLIT <= BIAS_LANES

F32 = jnp.float32
BF16 = jnp.bfloat16


def _const_spec(shape):
    nd = len(shape)
    return pl.BlockSpec(shape, lambda *_: (0,) * nd, pipeline_mode=pl.Buffered(1))


def _rms(x, g):
    return x * lax.rsqrt(jnp.mean(x * x, axis=-1, keepdims=True) + EPS) * g


def _sigmoid(x):
    return 1.0 / (1.0 + jnp.exp2(x * (-LOG2E)))


def _softplus(x):
    return jnp.maximum(x, 0.0) + jnp.log1p(jnp.exp(-jnp.abs(x)))


def _dot(a, b):
    return jnp.dot(a, b, preferred_element_type=F32)


def _dot_nt(a, b):
    return lax.dot_general(a, b, (((1,), (1,)), ((), ())), preferred_element_type=F32)


def _mem_kv_kernel(mem_ref, g_ref, w_ref, kv_ref):
    u = _rms(mem_ref[0], g_ref[0]).astype(BF16)
    kv_ref[0, 0] = _dot(u, w_ref[0]).astype(BF16)


def _mem_kv(mem, g_mem, w_mem_kv):
    depth, d_model, n_kv = w_mem_kv.shape
    bsz, n_mem, _ = mem.shape
    return pl.pallas_call(
        _mem_kv_kernel,
        grid=(depth, bsz),
        in_specs=[
            pl.BlockSpec((1, n_mem, d_model), lambda l, b: (b, 0, 0)),
            pl.BlockSpec((1, 1, d_model), lambda l, b: (l, 0, 0)),
            pl.BlockSpec((1, d_model, n_kv), lambda l, b: (l, 0, 0)),
        ],
        out_specs=pl.BlockSpec((1, 1, n_mem, n_kv), lambda l, b: (l, b, 0, 0)),
        out_shape=jax.ShapeDtypeStruct((depth, bsz, n_mem, n_kv), BF16),
        compiler_params=pltpu.CompilerParams(
            dimension_semantics=("arbitrary", "arbitrary"), vmem_limit_bytes=V7X_VMEM_LIMIT_BYTES),
        name="mem_kv",
    )(mem, g_mem.reshape(depth, 1, d_model), w_mem_kv.astype(BF16))


def _mixer_in_kernel(h_ref, gpre_ref, wlru_ref, wk_ref, wqvt_ref, wmq_ref, wgate_ref, wf_ref,
                     convw_ref, convb_ref, wri_ref, br_ref, bi_ref, lam_ref, bf_ref, kv_ref,
                     bgate_ref, wpa_ref, wpc_ref,
                     qt_ref, k_ref, vt_ref, kn_ref, part_ref, g1_ref,
                     conv_car, a_scr, b_scr, hcar, ccar, *, q_scale):
    tm, d = h_ref.shape[1], h_ref.shape[2]
    n_grp = tm // V7X_SUBLANES
    lru_blk = d // LRU_BLOCKS
    fox_dh = d // FOX_HEADS
    mem_dh = d // MEM_HEADS

    @pl.when(pl.program_id(1) == 0)
    def _():
        conv_car[...] = jnp.zeros_like(conv_car)
        hcar[...] = jnp.zeros_like(hcar)
        ccar[...] = jnp.zeros_like(ccar)

    u = _rms(h_ref[0], gpre_ref[...]).astype(BF16)

    z = _dot(u, wlru_ref[...])
    first_row = lax.broadcasted_iota(jnp.int32, z.shape, 0) == 0
    y = convw_ref[0:1, :] * z
    for kk in range(1, CONV_W):
        rot = pltpu.roll(y, 1, 0)
        y = jnp.where(first_row, conv_car[kk - 1:kk, :], rot) + convw_ref[kk:kk + 1, :] * z
        conv_car[kk - 1:kk, :] = rot[0:1, :]
    xc = y + convb_ref[...]
    xcb = xc.astype(BF16)
    neg_c_sp = (-LRU_C * LOG2E) * _softplus(-lam_ref[...])
    for hb in range(LRU_BLOCKS):
        sl = slice(hb * lru_blk, (hb + 1) * lru_blk)
        ri = _dot(xcb[:, sl], wri_ref[hb])
        r = _sigmoid(ri[:, :lru_blk] + br_ref[:, sl])
        i = _sigmoid(ri[:, lru_blk:] + bi_ref[:, sl])
        a = jnp.exp2(r * neg_c_sp[:, sl])
        a_scr[:, sl] = a
        b_scr[:, sl] = jnp.exp2(0.5 * jnp.log2(1.0 - a * a)) * i * xc[:, sl]
    a3 = a_scr[...].reshape(n_grp, V7X_SUBLANES, d)
    b3 = b_scr[...].reshape(n_grp, V7X_SUBLANES, d)
    sub = lax.broadcasted_iota(jnp.int32, a3.shape, 1)
    step = 1
    while step < V7X_SUBLANES:
        keep = sub >= step
        a_prev = jnp.where(keep, pltpu.roll(a3, step, 1), 1.0)
        b_prev = jnp.where(keep, pltpu.roll(b3, step, 1), 0.0)
        b3 = a3 * b_prev + b3
        a3 = a3 * a_prev
        step *= 2
    carry = hcar[...]
    for g in range(n_grp):
        hg = a3[g] * carry + b3[g]
        b_scr[g * V7X_SUBLANES:(g + 1) * V7X_SUBLANES, :] = hg
        carry = hg[V7X_SUBLANES - 1:V7X_SUBLANES, :]
    hcar[...] = carry
    pa = _dot(b_scr[...].astype(BF16), wpa_ref[...])

    kf = _dot(u, wk_ref[...])
    kz = kf.astype(BF16)
    kn_lane = lax.broadcasted_iota(jnp.int32, (1, BIAS_LANES), 1)
    kn = jnp.zeros((1, BIAS_LANES), F32)
    for hh in range(FOX_HEADS):
        kh = kf[:, hh * fox_dh:(hh + 1) * fox_dh]
        n2 = jnp.max(jnp.sum(kh * kh, axis=1, keepdims=True), axis=0, keepdims=True)
        kn = jnp.where(kn_lane == hh, n2, kn)
    kn_ref[0, 0] = kn
    zt = _dot_nt(wqvt_ref[...], u)
    qt = (zt[:d] * q_scale).astype(BF16)
    vt = zt[d:].astype(BF16)
    zf = _dot(u, wf_ref[...]) + bf_ref[...]
    lf = (jnp.minimum(zf, 0.0) - jnp.log1p(jnp.exp(-jnp.abs(zf)))) * LOG2E
    c3 = lf.reshape(n_grp, V7X_SUBLANES, BIAS_LANES)
    sub = lax.broadcasted_iota(jnp.int32, c3.shape, 1)
    step = 1
    while step < V7X_SUBLANES:
        c3 = c3 + jnp.where(sub >= step, pltpu.roll(c3, step, 1), 0.0)
        step *= 2
    ccarry = ccar[...]
    c_rows = []
    for g in range(n_grp):
        cg = c3[g] + ccarry
        c_rows.append(cg)
        ccarry = cg[V7X_SUBLANES - 1:V7X_SUBLANES, :]
    ccar[...] = ccarry
    c = jnp.concatenate(c_rows, axis=0)
    c_hi = c.astype(BF16).astype(F32)
    c_mid = (c - c_hi).astype(BF16).astype(F32)
    c_lo = (c - c_hi - c_mid).astype(BF16).astype(F32)
    lane = lax.broadcasted_iota(jnp.int32, c.shape, 1)
    piece = lane % N_SPLIT
    c_split = jnp.where(piece == 0, c_hi, jnp.where(piece == 1, c_mid, c_lo))
    n_bias = FOX_HEADS * N_SPLIT
    k_bias = jnp.where(lane < n_bias, -c_split,
                       jnp.where(lane < REF_LANE0 + N_SPLIT, 1.0, 0.0)).astype(BF16)
    c_split_t = pltpu.roll(c_split.T, n_bias, 0)
    row = lax.broadcasted_iota(jnp.int32, c_split_t.shape, 0)
    for hh in range(FOX_HEADS):
        sl = slice(hh * fox_dh, (hh + 1) * fox_dh)
        lo = hh * N_SPLIT
        ones_rows = (row >= lo) & (row < lo + N_SPLIT)
        c_rows_sel = (row >= n_bias + lo) & (row < n_bias + lo + N_SPLIT)
        q_bias = jnp.where(ones_rows, 1.0, jnp.where(c_rows_sel, c_split_t, 0.0)).astype(BF16)
        k_ref[0, hh, 0, :, 0:fox_dh] = kz[:, sl]
        k_ref[0, hh, 0, :, fox_dh:fox_dh + BIAS_LANES] = k_bias
        qt_ref[0, hh, 0:fox_dh, :] = qt[sl, :]
        qt_ref[0, hh, fox_dh:fox_dh + BIAS_LANES, :] = q_bias
        vt_ref[0, hh, 0] = vt[sl, :]

    mq = (_dot(u, wmq_ref[...]) * (mem_dh ** -0.5)).astype(BF16)
    yc = []
    for hh in range(MEM_HEADS):
        sl = slice(hh * mem_dh, (hh + 1) * mem_dh)
        lg = _dot_nt(mq[:, sl], kv_ref[0, :, sl])
        e = jnp.exp(lg - jnp.max(lg, axis=-1, keepdims=True))
        num = _dot(e.astype(BF16), kv_ref[0, :, d + hh * mem_dh:d + (hh + 1) * mem_dh])
        yc.append((num / jnp.sum(e, axis=-1, keepdims=True)).astype(BF16))
    pc = _dot(jnp.concatenate(yc, axis=1), wpc_ref[...])

    g0 = _sigmoid(_dot(u, wgate_ref[:, 0:d]) + bgate_ref[0:1, :])
    g2 = _sigmoid(_dot(u, wgate_ref[:, 2 * d:3 * d]) + bgate_ref[2:3, :])
    part_ref[0] = g0 * pa + g2 * pc
    g1_ref[0] = _sigmoid(_dot(u, wgate_ref[:, d:2 * d]) + bgate_ref[1:2, :])


def _mixer_in(h, kv, p):
    bsz, seq, d = h.shape
    tm = ROW_TILE
    n_t = seq // tm
    fox_dh = d // FOX_HEADS
    n_mem, n_kv = kv.shape[1], kv.shape[2]
    qk_dim = fox_dh + BIAS_LANES
    row_spec = pl.BlockSpec((1, tm, d), lambda b, j: (b, j, 0))
    in_specs = [
        row_spec,
        _const_spec((1, d)),
        _const_spec((d, d)), _const_spec((d, d)), _const_spec((2 * d, d)), _const_spec((d, d)),
        _const_spec((d, N_BRANCH * d)), _const_spec((d, BIAS_LANES)),
        _const_spec((CONV_W, d)), _const_spec((1, d)),
        _const_spec((LRU_BLOCKS, d // LRU_BLOCKS, 2 * d // LRU_BLOCKS)),
        _const_spec((1, d)), _const_spec((1, d)), _const_spec((1, d)), _const_spec((1, BIAS_LANES)),
        pl.BlockSpec((1, n_mem, n_kv), lambda b, j: (b, 0, 0)),
        _const_spec((N_BRANCH, d)), _const_spec((d, d)), _const_spec((d, d)),
    ]
    out_specs = [
        pl.BlockSpec((1, FOX_HEADS, qk_dim, tm), lambda b, j: (b, 0, 0, j)),
        pl.BlockSpec((1, FOX_HEADS, 1, tm, qk_dim), lambda b, j: (b, 0, j, 0, 0)),
        pl.BlockSpec((1, FOX_HEADS, 1, fox_dh, tm), lambda b, j: (b, 0, j, 0, 0)),
        pl.BlockSpec((1, 1, 1, BIAS_LANES), lambda b, j: (b, j, 0, 0)),
        row_spec, row_spec,
    ]
    out_shape = [
        jax.ShapeDtypeStruct((bsz, FOX_HEADS, qk_dim, seq), BF16),
        jax.ShapeDtypeStruct((bsz, FOX_HEADS, n_t, tm, qk_dim), BF16),
        jax.ShapeDtypeStruct((bsz, FOX_HEADS, n_t, fox_dh, tm), BF16),
        jax.ShapeDtypeStruct((bsz, n_t, 1, BIAS_LANES), F32),
        jax.ShapeDtypeStruct((bsz, seq, d), F32),
        jax.ShapeDtypeStruct((bsz, seq, d), F32),
    ]
    scratch = [
        pltpu.VMEM((CONV_W - 1, d), F32),
        pltpu.VMEM((tm, d), F32), pltpu.VMEM((tm, d), F32),
        pltpu.VMEM((1, d), F32), pltpu.VMEM((1, BIAS_LANES), F32),
    ]
    return pl.pallas_call(
        functools.partial(_mixer_in_kernel, q_scale=(fox_dh ** -0.5) * LOG2E),
        grid=(bsz, n_t),
        in_specs=in_specs, out_specs=out_specs, out_shape=out_shape, scratch_shapes=scratch,
        compiler_params=pltpu.CompilerParams(
            dimension_semantics=("arbitrary", "arbitrary"), vmem_limit_bytes=V7X_VMEM_LIMIT_BYTES),
        name="mixer_in",
    )(h, p["g_pre"], p["w_lru"], p["w_k"], p["w_qvt"], p["w_mq"], p["w_gate"], p["w_f"],
      p["conv_w"], p["conv_b"], p["w_ri"], p["b_r"], p["b_i"], p["lam"], p["b_f"], kv,
      p["b_gate"], p["w_pa"], p["w_pc"])


def _fox_kernel(qt_ref, k_ref, vt_ref, kn_ref, o_ref, q_eff, p0, p1, acc_scr, l_scr, m_scr):
    tq = qt_ref.shape[3]
    tc = k_ref.shape[3]
    dh = o_ref.shape[2]
    per_q = tq // tc
    i = pl.program_id(2)
    p_buf = (p0, p1)
    ref_row0 = dh + REF_LANE0

    kn = jnp.max(kn_ref[0], axis=0)
    head_lane = lax.broadcasted_iota(jnp.int32, kn.shape, 1) == pl.program_id(1)
    kmax2 = jnp.max(jnp.where(head_lane, kn, 0.0), axis=1, keepdims=True)
    q = qt_ref[0, 0]
    qf = q[0:dh].astype(F32)
    ref = jnp.sqrt(jnp.sum(qf * qf, axis=0, keepdims=True) * kmax2) * REF_SLACK
    r_hi = ref.astype(BF16).astype(F32)
    r_mid = (ref - r_hi).astype(BF16).astype(F32)
    r_lo = (ref - r_hi - r_mid).astype(BF16).astype(F32)
    row = lax.broadcasted_iota(jnp.int32, (V7X_BF16_SUBLANES, tq), 0)
    q_eff[...] = q
    q_eff[ref_row0:ref_row0 + V7X_BF16_SUBLANES, :] = jnp.where(
        row == 0, -r_hi, jnp.where(row == 1, -r_mid, jnp.where(row == 2, -r_lo, 0.0))).astype(BF16)

    def probs_diag():
        l_new = None
        for sub in range(per_q):
            c0 = sub * tc
            s = _dot(k_ref[0, 0, i * per_q + sub], q_eff[:, c0:tq])
            key = lax.broadcasted_iota(jnp.int32, s.shape, 0)
            qry = lax.broadcasted_iota(jnp.int32, s.shape, 1)
            p = jnp.exp2(jnp.where(key <= qry, s, NEG_BIG))
            p0[c0:c0 + tc, c0:tq] = p.astype(BF16)
            l_sub = jnp.sum(p, axis=0, keepdims=True)
            if sub > 0:
                l_sub = jnp.concatenate([jnp.zeros((1, c0), F32), l_sub], axis=1)
            l_new = l_sub if l_new is None else l_new + l_sub
        l_scr[...] = l_new

    def accumulate_diag():
        upd = None
        for sub in range(per_q):
            c0 = sub * tc
            term = _dot(vt_ref[0, 0, i * per_q + sub], p0[c0:c0 + tc, c0:tq])
            if sub > 0:
                term = jnp.concatenate([jnp.zeros((dh, c0), F32), term], axis=1)
            upd = term if upd is None else upd + term
        acc_scr[...] = upd

    def probs(tile, slot):
        l_new = l_scr[...]
        for sub in range(per_q):
            p = jnp.exp2(_dot(k_ref[0, 0, tile * per_q + sub], q_eff[...]))
            l_new = l_new + jnp.sum(p, axis=0, keepdims=True)
            p_buf[slot][sub * tc:(sub + 1) * tc, :] = p.astype(BF16)
        l_scr[...] = l_new

    def accumulate(tile, slot):
        upd = None
        for sub in range(per_q):
            term = _dot(vt_ref[0, 0, tile * per_q + sub], p_buf[slot][sub * tc:(sub + 1) * tc, :])
            upd = term if upd is None else upd + term
        acc_scr[...] += upd

    def by_parity(x, fn):
        for par in range(2):
            pl.when(x % 2 == par)(functools.partial(fn, par))

    probs_diag()

    @pl.when(i == 0)
    def _():
        accumulate_diag()

    @pl.when(i > 0)
    def _():
        probs(0, 1)
        accumulate_diag()

        def step(t, carry):
            def one(par):
                probs(t, 1 - par)
                accumulate(t - 1, par)
            by_parity(t, one)
            return carry

        lax.fori_loop(1, i, step, 0)
        by_parity(i, lambda par: accumulate(i - 1, par))

    @pl.when(jnp.logical_not(jnp.min(l_scr[...]) >= L_UNDERFLOW))
    def _():
        m_scr[...] = jnp.full_like(m_scr, NEG_BIG)
        l_scr[...] = jnp.zeros_like(l_scr)
        acc_scr[...] = jnp.zeros_like(acc_scr)

        def chunk(c, carry):
            s = _dot(k_ref[0, 0, c], qt_ref[0, 0])
            key = lax.broadcasted_iota(jnp.int32, s.shape, 0) + c * tc
            qry = lax.broadcasted_iota(jnp.int32, s.shape, 1) + i * tq
            s = jnp.where(key <= qry, s, NEG_BIG)
            m_old = m_scr[...]
            m_new = jnp.maximum(m_old, jnp.max(s, axis=0, keepdims=True))
            p = jnp.exp2(s - m_new)
            alpha = jnp.exp2(m_old - m_new)
            l_scr[...] = alpha * l_scr[...] + jnp.sum(p, axis=0, keepdims=True)
            acc_scr[...] = alpha * acc_scr[...] + _dot(vt_ref[0, 0, c], p.astype(BF16))
            m_scr[...] = m_new
            return carry

        lax.fori_loop(0, (i + 1) * per_q, chunk, 0)

    o_ref[0] = (acc_scr[...] / l_scr[...]).T.astype(o_ref.dtype)


def _fox_attention(qt, k, vt, kn, d_model):
    bsz, n_heads, qk_dim, seq = qt.shape
    n_c, tc = k.shape[2], k.shape[3]
    dh = vt.shape[3]
    tq = Q_TILE
    resident = lambda shape: pl.BlockSpec(shape, lambda b, h, i: (b, h, 0, 0, 0))
    return pl.pallas_call(
        _fox_kernel,
        grid=(bsz, n_heads, seq // tq),
        in_specs=[
            pl.BlockSpec((1, 1, qk_dim, tq), lambda b, h, i: (b, h, 0, i)),
            resident((1, 1, n_c, tc, qk_dim)),
            resident((1, 1, n_c, dh, tc)),
            pl.BlockSpec((1,) + kn.shape[1:], lambda b, h, i: (b, 0, 0, 0)),
        ],
        out_specs=pl.BlockSpec((1, tq, dh), lambda b, h, i: (b, i, h)),
        out_shape=jax.ShapeDtypeStruct((bsz, seq, d_model), BF16),
        scratch_shapes=[
            pltpu.VMEM((qk_dim, tq), BF16),
            pltpu.VMEM((tq, tq), BF16), pltpu.VMEM((tq, tq), BF16),
            pltpu.VMEM((dh, tq), F32), pltpu.VMEM((1, tq), F32),
            pltpu.VMEM((1, tq), F32)],
        compiler_params=pltpu.CompilerParams(
            dimension_semantics=("arbitrary", "arbitrary", "arbitrary"),
            vmem_limit_bytes=V7X_VMEM_LIMIT_BYTES),
        name="fox_attention",
    )(qt, k, vt, kn)


def _post_kernel(yb_ref, part_ref, g1_ref, h_ref, wpb_ref, wo_ref, gpost_ref, gmlp_ref, wup_ref,
                 wdown_ref, gmlp_post_ref, out_ref):
    merged = part_ref[0] + g1_ref[0] * _dot(yb_ref[0], wpb_ref[...])
    h1 = h_ref[0] + _rms(_dot(merged.astype(BF16), wo_ref[...]), gpost_ref[...])
    u = _rms(h1, gmlp_ref[...]).astype(BF16)
    act = jnp.square(jnp.maximum(_dot(u, wup_ref[...]), 0.0))
    out_ref[0] = h1 + _rms(_dot(act.astype(BF16), wdown_ref[...]), gmlp_post_ref[...])


def _post(yb, part, g1, h, p):
    bsz, seq, d = h.shape
    d_ff = p["w_up"].shape[1]
    tm = POST_ROW_TILE
    row_spec = pl.BlockSpec((1, tm, d), lambda b, j: (b, j, 0))
    return pl.pallas_call(
        _post_kernel,
        grid=(bsz, seq // tm),
        in_specs=[row_spec, row_spec, row_spec, row_spec,
                  _const_spec((d, d)), _const_spec((d, d)), _const_spec((1, d)), _const_spec((1, d)),
                  _const_spec((d, d_ff)), _const_spec((d_ff, d)), _const_spec((1, d))],
        out_specs=row_spec,
        out_shape=jax.ShapeDtypeStruct((bsz, seq, d), F32),
        compiler_params=pltpu.CompilerParams(
            dimension_semantics=("arbitrary", "arbitrary"), vmem_limit_bytes=V7X_VMEM_LIMIT_BYTES),
        name="post",
    )(yb, part, g1, h, p["w_pb"], p["w_o"], p["g_post"], p["g_mlp_pre"], p["w_up"], p["w_down"],
      p["g_mlp_post"])


def _layer_params(l, d, g_mix_pre, w_in, conv_w, conv_b, w_r, b_r, w_i, b_i, lru_lambda, b_f,
                  b_gate, w_pa, w_pb, w_pc, w_o, g_mix_post, g_mlp_pre, w_up, w_down, g_mlp_post):
    wi = w_in[l]
    off_q, off_k, off_v, off_mq, off_gate = d, 2 * d, 3 * d, 4 * d, 5 * d
    off_f = off_gate + N_BRANCH * d
    w_f = jnp.repeat(wi[:, off_f:off_f + FOX_HEADS], N_SPLIT, axis=1)
    pad = BIAS_LANES - FOX_HEADS * N_SPLIT
    row = lambda v: v.reshape(1, -1).astype(F32)
    return dict(
        g_pre=row(g_mix_pre[l]),
        w_lru=wi[:, 0:d].astype(BF16),
        w_k=wi[:, off_k:off_k + d].astype(BF16),
        w_qvt=jnp.concatenate([wi[:, off_q:off_q + d].T, wi[:, off_v:off_v + d].T], axis=0).astype(BF16),
        w_mq=wi[:, off_mq:off_mq + d].astype(BF16),
        w_gate=wi[:, off_gate:off_gate + N_BRANCH * d].astype(BF16),
        w_f=jnp.pad(w_f, ((0, 0), (0, pad))).astype(BF16),
        conv_w=conv_w[l].astype(F32), conv_b=row(conv_b[l]),
        w_ri=jnp.concatenate([w_r[l], w_i[l]], axis=-1).astype(BF16),
        b_r=row(b_r[l]), b_i=row(b_i[l]), lam=row(lru_lambda[l]),
        b_f=jnp.pad(jnp.repeat(b_f[l], N_SPLIT), (0, pad)).reshape(1, -1).astype(F32),
        b_gate=b_gate[l].astype(F32),
        w_pa=w_pa[l].astype(BF16), w_pb=w_pb[l].astype(BF16), w_pc=w_pc[l].astype(BF16),
        w_o=w_o[l].astype(BF16), g_post=row(g_mix_post[l]), g_mlp_pre=row(g_mlp_pre[l]),
        w_up=w_up[l].astype(BF16), w_down=w_down[l].astype(BF16), g_mlp_post=row(g_mlp_post[l]),
    )


def kernel(x, mem, g_mix_pre, w_in, conv_w, conv_b, w_r, b_r, w_i, b_i, lru_lambda, b_f, g_mem, w_mem_kv, b_gate, w_pa, w_pb, w_pc, w_o, g_mix_post, g_mlp_pre, w_up, w_down, g_mlp_post):
    bsz, seq, d = x.shape
    depth = w_in.shape[0]
    assert seq % Q_TILE == 0 and Q_TILE % ROW_TILE == 0 and d % (FOX_HEADS * V7X_LANES) == 0
    kv_all = _mem_kv(mem, g_mem, w_mem_kv)
    h = x
    for l in range(depth):
        p = _layer_params(l, d, g_mix_pre, w_in, conv_w, conv_b, w_r, b_r, w_i, b_i, lru_lambda,
                          b_f, b_gate, w_pa, w_pb, w_pc, w_o, g_mix_post, g_mlp_pre, w_up, w_down,
                          g_mlp_post)
        qt, k, vt, kn, part, g1 = _mixer_in(h, kv_all[l], p)
        yb = _fox_attention(qt, k, vt, kn, d)
        h = _post(yb, part, g1, h, p)
    return h
```

```python
import functools

import jax
import jax.numpy as jnp
from jax import lax
from jax.experimental import pallas as pl
from jax.experimental.pallas import tpu as pltpu

EPS = 1e-6
LRU_BLOCKS = 8
LRU_C = 8.0
CONV_W = 4
FOX_HEADS = 8
MEM_HEADS = 4
N_BRANCH = 3
LOG2E = 1.4426950408889634
NEG_BIG = -1e30

V7X_LANES = 128
V7X_SUBLANES = 8
V7X_BF16_SUBLANES = 16
V7X_VMEM_LIMIT_BYTES = 56 * 1024 * 1024

ROW_TILE = 512
POST_ROW_TILE = 512
Q_TILE = 2048
BIAS_LANES = V7X_LANES
N_SPLIT = 3
REF_LANE0 = 2 * FOX_HEADS * N_SPLIT
REF_SLACK = 1.01
L_UNDERFLOW = 2.0 ** -60
assert REF_LANE0 % V7X_BF16_SUBLANES == 0 and REF_LANE0 + N_SPLIT <= BIAS_LANES

F32 = jnp.float32
BF16 = jnp.bfloat16


def _const_spec(shape):
    nd = len(shape)
    return pl.BlockSpec(shape, lambda *_: (0,) * nd, pipeline_mode=pl.Buffered(1))


def _rms(x, g):
    return x * lax.rsqrt(jnp.mean(x * x, axis=-1, keepdims=True) + EPS) * g


def _sigmoid(x):
    return 1.0 / (1.0 + jnp.exp2(x * (-LOG2E)))


def _softplus(x):
    return jnp.maximum(x, 0.0) + jnp.log1p(jnp.exp(-jnp.abs(x)))


def _dot(a, b):
    return jnp.dot(a, b, preferred_element_type=F32)


def _dot_nt(a, b):
    return lax.dot_general(a, b, (((1,), (1,)), ((), ())), preferred_element_type=F32)


def _mem_kv_kernel(mem_ref, g_ref, w_ref, kv_ref):
    u = _rms(mem_ref[0], g_ref[0]).astype(BF16)
    kv_ref[0, 0] = _dot(u, w_ref[0]).astype(BF16)


def _mem_kv(mem, g_mem, w_mem_kv):
    depth, d_model, n_kv = w_mem_kv.shape
    bsz, n_mem, _ = mem.shape
    return pl.pallas_call(
        _mem_kv_kernel,
        grid=(depth, bsz),
        in_specs=[
            pl.BlockSpec((1, n_mem, d_model), lambda l, b: (b, 0, 0)),
            pl.BlockSpec((1, 1, d_model), lambda l, b: (l, 0, 0)),
            pl.BlockSpec((1, d_model, n_kv), lambda l, b: (l, 0, 0)),
        ],
        out_specs=pl.BlockSpec((1, 1, n_mem, n_kv), lambda l, b: (l, b, 0, 0)),
        out_shape=jax.ShapeDtypeStruct((depth, bsz, n_mem, n_kv), BF16),
        compiler_params=pltpu.CompilerParams(
            dimension_semantics=("arbitrary", "arbitrary"), vmem_limit_bytes=V7X_VMEM_LIMIT_BYTES),
        name="mem_kv",
    )(mem, g_mem.reshape(depth, 1, d_model), w_mem_kv.astype(BF16))


def _mixer_in_kernel(h_ref, gpre_ref, wlru_ref, wk_ref, wqvt_ref, wmq_ref, wgate_ref, wf_ref,
                     convw_ref, convb_ref, wri_ref, br_ref, bi_ref, lam_ref, bf_ref, kv_ref,
                     bgate_ref, wpa_ref, wpc_ref,
                     qt_ref, k_ref, vt_ref, kn_ref, part_ref, g1_ref,
                     conv_car, a_scr, b_scr, hcar, ccar, *, q_scale):
    tm, d = h_ref.shape[1], h_ref.shape[2]
    n_grp = tm // V7X_SUBLANES
    lru_blk = d // LRU_BLOCKS
    fox_dh = d // FOX_HEADS
    mem_dh = d // MEM_HEADS

    @pl.when(pl.program_id(1) == 0)
    def _():
        conv_car[...] = jnp.zeros_like(conv_car)
        hcar[...] = jnp.zeros_like(hcar)
        ccar[...] = jnp.zeros_like(ccar)

    u = _rms(h_ref[0], gpre_ref[...]).astype(BF16)

    z = _dot(u, wlru_ref[...])
    first_row = lax.broadcasted_iota(jnp.int32, z.shape, 0) == 0
    y = convw_ref[0:1, :] * z
    for kk in range(1, CONV_W):
        rot = pltpu.roll(y, 1, 0)
        y = jnp.where(first_row, conv_car[kk - 1:kk, :], rot) + convw_ref[kk:kk + 1, :] * z
        conv_car[kk - 1:kk, :] = rot[0:1, :]
    xc = y + convb_ref[...]
    xcb = xc.astype(BF16)
    neg_c_sp = (-LRU_C * LOG2E) * _softplus(-lam_ref[...])
    for hb in range(LRU_BLOCKS):
        sl = slice(hb * lru_blk, (hb + 1) * lru_blk)
        ri = _dot(xcb[:, sl], wri_ref[hb])
        r = _sigmoid(ri[:, :lru_blk] + br_ref[:, sl])
        i = _sigmoid(ri[:, lru_blk:] + bi_ref[:, sl])
        a = jnp.exp2(r * neg_c_sp[:, sl])
        a_scr[:, sl] = a
        b_scr[:, sl] = jnp.exp2(0.5 * jnp.log2(1.0 - a * a)) * i * xc[:, sl]
    a3 = a_scr[...].reshape(n_grp, V7X_SUBLANES, d)
    b3 = b_scr[...].reshape(n_grp, V7X_SUBLANES, d)
    sub = lax.broadcasted_iota(jnp.int32, a3.shape, 1)
    step = 1
    while step < V7X_SUBLANES:
        keep = sub >= step
        a_prev = jnp.where(keep, pltpu.roll(a3, step, 1), 1.0)
        b_prev = jnp.where(keep, pltpu.roll(b3, step, 1), 0.0)
        b3 = a3 * b_prev + b3
        a3 = a3 * a_prev
        step *= 2
    carry = hcar[...]
    for g in range(n_grp):
        hg = a3[g] * carry + b3[g]
        b_scr[g * V7X_SUBLANES:(g + 1) * V7X_SUBLANES, :] = hg
        carry = hg[V7X_SUBLANES - 1:V7X_SUBLANES, :]
    hcar[...] = carry
    pa = _dot(b_scr[...].astype(BF16), wpa_ref[...])

    kf = _dot(u, wk_ref[...])
    kz = kf.astype(BF16)
    kn_lane = lax.broadcasted_iota(jnp.int32, (1, BIAS_LANES), 1)
    kn = jnp.zeros((1, BIAS_LANES), F32)
    for hh in range(FOX_HEADS):
        kh = kf[:, hh * fox_dh:(hh + 1) * fox_dh]
        n2 = jnp.max(jnp.sum(kh * kh, axis=1, keepdims=True), axis=0, keepdims=True)
        kn = jnp.where(kn_lane == hh, n2, kn)
    kn_ref[0, 0] = kn
    zt = _dot_nt(wqvt_ref[...], u)
    qt = (zt[:d] * q_scale).astype(BF16)
    vt = zt[d:].astype(BF16)
    zf = _dot(u, wf_ref[...]) + bf_ref[...]
    lf = (jnp.minimum(zf, 0.0) - jnp.log1p(jnp.exp(-jnp.abs(zf)))) * LOG2E
    c3 = lf.reshape(n_grp, V7X_SUBLANES, BIAS_LANES)
    sub = lax.broadcasted_iota(jnp.int32, c3.shape, 1)
    step = 1
    while step < V7X_SUBLANES:
        c3 = c3 + jnp.where(sub >= step, pltpu.roll(c3, step, 1), 0.0)
        step *= 2
    ccarry = ccar[...]
    c_rows = []
    for g in range(n_grp):
        cg = c3[g] + ccarry
        c_rows.append(cg)
        ccarry = cg[V7X_SUBLANES - 1:V7X_SUBLANES, :]
    ccar[...] = ccarry
    c = jnp.concatenate(c_rows, axis=0)
    c_hi = c.astype(BF16).astype(F32)
    c_mid = (c - c_hi).astype(BF16).astype(F32)
    c_lo = (c - c_hi - c_mid).astype(BF16).astype(F32)
    lane = lax.broadcasted_iota(jnp.int32, c.shape, 1)
    piece = lane % N_SPLIT
    c_split = jnp.where(piece == 0, c_hi, jnp.where(piece == 1, c_mid, c_lo))
    n_bias = FOX_HEADS * N_SPLIT
    k_bias = jnp.where(lane < n_bias, -c_split,
                       jnp.where(lane < REF_LANE0 + N_SPLIT, 1.0, 0.0)).astype(BF16)
    c_split_t = pltpu.roll(c_split.T, n_bias, 0)
    row = lax.broadcasted_iota(jnp.int32, c_split_t.shape, 0)
    for hh in range(FOX_HEADS):
        sl = slice(hh * fox_dh, (hh + 1) * fox_dh)
        lo = hh * N_SPLIT
        ones_rows = (row >= lo) & (row < lo + N_SPLIT)
        c_rows_sel = (row >= n_bias + lo) & (row < n_bias + lo + N_SPLIT)
        q_bias = jnp.where(ones_rows, 1.0, jnp.where(c_rows_sel, c_split_t, 0.0)).astype(BF16)
        k_ref[0, hh, 0, :, 0:fox_dh] = kz[:, sl]
        k_ref[0, hh, 0, :, fox_dh:fox_dh + BIAS_LANES] = k_bias
        qt_ref[0, hh, 0:fox_dh, :] = qt[sl, :]
        qt_ref[0, hh, fox_dh:fox_dh + BIAS_LANES, :] = q_bias
        vt_ref[0, hh, 0] = vt[sl, :]

    mq = (_dot(u, wmq_ref[...]) * (mem_dh ** -0.5)).astype(BF16)
    yc = []
    for hh in range(MEM_HEADS):
        sl = slice(hh * mem_dh, (hh + 1) * mem_dh)
        lg = _dot_nt(mq[:, sl], kv_ref[0, :, sl])
        e = jnp.exp(lg - jnp.max(lg, axis=-1, keepdims=True))
        num = _dot(e.astype(BF16), kv_ref[0, :, d + hh * mem_dh:d + (hh + 1) * mem_dh])
        yc.append((num / jnp.sum(e, axis=-1, keepdims=True)).astype(BF16))
    pc = _dot(jnp.concatenate(yc, axis=1), wpc_ref[...])

    g0 = _sigmoid(_dot(u, wgate_ref[:, 0:d]) + bgate_ref[0:1, :])
    g2 = _sigmoid(_dot(u, wgate_ref[:, 2 * d:3 * d]) + bgate_ref[2:3, :])
    part_ref[0] = g0 * pa + g2 * pc
    g1_ref[0] = _sigmoid(_dot(u, wgate_ref[:, d:2 * d]) + bgate_ref[1:2, :])


def _mixer_in(h, kv, p):
    bsz, seq, d = h.shape
    tm = ROW_TILE
    n_t = seq // tm
    fox_dh = d // FOX_HEADS
    n_mem, n_kv = kv.shape[1], kv.shape[2]
    qk_dim = fox_dh + BIAS_LANES
    row_spec = pl.BlockSpec((1, tm, d), lambda b, j: (b, j, 0))
    in_specs = [
        row_spec,
        _const_spec((1, d)),
        _const_spec((d, d)), _const_spec((d, d)), _const_spec((2 * d, d)), _const_spec((d, d)),
        _const_spec((d, N_BRANCH * d)), _const_spec((d, BIAS_LANES)),
        _const_spec((CONV_W, d)), _const_spec((1, d)),
        _const_spec((LRU_BLOCKS, d // LRU_BLOCKS, 2 * d // LRU_BLOCKS)),
        _const_spec((1, d)), _const_spec((1, d)), _const_spec((1, d)), _const_spec((1, BIAS_LANES)),
        pl.BlockSpec((1, n_mem, n_kv), lambda b, j: (b, 0, 0)),
        _const_spec((N_BRANCH, d)), _const_spec((d, d)), _const_spec((d, d)),
    ]
    out_specs = [
        pl.BlockSpec((1, FOX_HEADS, qk_dim, tm), lambda b, j: (b, 0, 0, j)),
        pl.BlockSpec((1, FOX_HEADS, 1, tm, qk_dim), lambda b, j: (b, 0, j, 0, 0)),
        pl.BlockSpec((1, FOX_HEADS, 1, fox_dh, tm), lambda b, j: (b, 0, j, 0, 0)),
        pl.BlockSpec((1, 1, 1, BIAS_LANES), lambda b, j: (b, j, 0, 0)),
        row_spec, row_spec,
    ]
    out_shape = [
        jax.ShapeDtypeStruct((bsz, FOX_HEADS, qk_dim, seq), BF16),
        jax.ShapeDtypeStruct((bsz, FOX_HEADS, n_t, tm, qk_dim), BF16),
        jax.ShapeDtypeStruct((bsz, FOX_HEADS, n_t, fox_dh, tm), BF16),
        jax.ShapeDtypeStruct((bsz, n_t, 1, BIAS_LANES), F32),
        jax.ShapeDtypeStruct((bsz, seq, d), F32),
        jax.ShapeDtypeStruct((bsz, seq, d), F32),
    ]
    scratch = [
        pltpu.VMEM((CONV_W - 1, d), F32),
        pltpu.VMEM((tm, d), F32), pltpu.VMEM((tm, d), F32),
        pltpu.VMEM((1, d), F32), pltpu.VMEM((1, BIAS_LANES), F32),
    ]
    return pl.pallas_call(
        functools.partial(_mixer_in_kernel, q_scale=(fox_dh ** -0.5) * LOG2E),
        grid=(bsz, n_t),
        in_specs=in_specs, out_specs=out_specs, out_shape=out_shape, scratch_shapes=scratch,
        compiler_params=pltpu.CompilerParams(
            dimension_semantics=("arbitrary", "arbitrary"), vmem_limit_bytes=V7X_VMEM_LIMIT_BYTES),
        name="mixer_in",
    )(h, p["g_pre"], p["w_lru"], p["w_k"], p["w_qvt"], p["w_mq"], p["w_gate"], p["w_f"],
      p["conv_w"], p["conv_b"], p["w_ri"], p["b_r"], p["b_i"], p["lam"], p["b_f"], kv,
      p["b_gate"], p["w_pa"], p["w_pc"])


def _fox_kernel(qt_ref, k_ref, vt_ref, kn_ref, o_ref, q_eff, p0, p1, acc_scr, l_scr, m_scr):
    tq = qt_ref.shape[3]
    tc = k_ref.shape[3]
    dh = o_ref.shape[2]
    per_q = tq // tc
    i = pl.program_id(2)
    p_buf = (p0, p1)
    ref_row0 = dh + REF_LANE0

    kn = jnp.max(kn_ref[0], axis=0)
    head_lane = lax.broadcasted_iota(jnp.int32, kn.shape, 1) == pl.program_id(1)
    kmax2 = jnp.max(jnp.where(head_lane, kn, 0.0), axis=1, keepdims=True)
    q = qt_ref[0, 0]
    qf = q[0:dh].astype(F32)
    ref = jnp.sqrt(jnp.sum(qf * qf, axis=0, keepdims=True) * kmax2) * REF_SLACK
    r_hi = ref.astype(BF16).astype(F32)
    r_mid = (ref - r_hi).astype(BF16).astype(F32)
    r_lo = (ref - r_hi - r_mid).astype(BF16).astype(F32)
    row = lax.broadcasted_iota(jnp.int32, (V7X_BF16_SUBLANES, tq), 0)
    q_eff[...] = q
    q_eff[ref_row0:ref_row0 + V7X_BF16_SUBLANES, :] = jnp.where(
        row == 0, -r_hi, jnp.where(row == 1, -r_mid, jnp.where(row == 2, -r_lo, 0.0))).astype(BF16)

    def probs_diag():
        l_new = None
        for sub in range(per_q):
            c0 = sub * tc
            s = _dot(k_ref[0, 0, i * per_q + sub], q_eff[:, c0:tq])
            key = lax.broadcasted_iota(jnp.int32, s.shape, 0)
            qry = lax.broadcasted_iota(jnp.int32, s.shape, 1)
            p = jnp.exp2(jnp.where(key <= qry, s, NEG_BIG))
            p0[c0:c0 + tc, c0:tq] = p.astype(BF16)
            l_sub = jnp.sum(p, axis=0, keepdims=True)
            if sub > 0:
                l_sub = jnp.concatenate([jnp.zeros((1, c0), F32), l_sub], axis=1)
            l_new = l_sub if l_new is None else l_new + l_sub
        l_scr[...] = l_new

    def accumulate_diag(subs):
        upd = None
        for sub in subs:
            c0 = sub * tc
            term = _dot(vt_ref[0, 0, i * per_q + sub], p0[c0:c0 + tc, c0:tq])
            if sub > 0:
                term = jnp.concatenate([jnp.zeros((dh, c0), F32), term], axis=1)
            upd = term if upd is None else upd + term
        if subs[0] == 0:
            acc_scr[...] = upd
        else:
            acc_scr[...] += upd

    def probs(tile, slot):
        l_new = l_scr[...]
        for sub in range(per_q):
            p = jnp.exp2(_dot(k_ref[0, 0, tile * per_q + sub], q_eff[...]))
            l_new = l_new + jnp.sum(p, axis=0, keepdims=True)
            p_buf[slot][sub * tc:(sub + 1) * tc, :] = p.astype(BF16)
        l_scr[...] = l_new

    def accumulate(tile, slot):
        upd = None
        for sub in range(per_q):
            term = _dot(vt_ref[0, 0, tile * per_q + sub], p_buf[slot][sub * tc:(sub + 1) * tc, :])
            upd = term if upd is None else upd + term
        acc_scr[...] += upd

    def by_parity(x, fn):
        for par in range(2):
            pl.when(x % 2 == par)(functools.partial(fn, par))

    diag_early, diag_late = range(0, per_q // 2), range(per_q // 2, per_q)
    probs_diag()
    accumulate_diag(diag_early)

    @pl.when(i == 0)
    def _():
        accumulate_diag(diag_late)

    @pl.when(i > 0)
    def _():
        probs(0, 1)
        accumulate_diag(diag_late)

        def step(t, carry):
            def one(par):
                probs(t, 1 - par)
                accumulate(t - 1, par)
            by_parity(t, one)
            return carry

        lax.fori_loop(1, i, step, 0)
        by_parity(i, lambda par: accumulate(i - 1, par))

    @pl.when(jnp.logical_not(jnp.min(l_scr[...]) >= L_UNDERFLOW))
    def _():
        m_scr[...] = jnp.full_like(m_scr, NEG_BIG)
        l_scr[...] = jnp.zeros_like(l_scr)
        acc_scr[...] = jnp.zeros_like(acc_scr)

        def chunk(c, carry):
            s = _dot(k_ref[0, 0, c], qt_ref[0, 0])
            key = lax.broadcasted_iota(jnp.int32, s.shape, 0) + c * tc
            qry = lax.broadcasted_iota(jnp.int32, s.shape, 1) + i * tq
            s = jnp.where(key <= qry, s, NEG_BIG)
            m_old = m_scr[...]
            m_new = jnp.maximum(m_old, jnp.max(s, axis=0, keepdims=True))
            p = jnp.exp2(s - m_new)
            alpha = jnp.exp2(m_old - m_new)
            l_scr[...] = alpha * l_scr[...] + jnp.sum(p, axis=0, keepdims=True)
            acc_scr[...] = alpha * acc_scr[...] + _dot(vt_ref[0, 0, c], p.astype(BF16))
            m_scr[...] = m_new
            return carry

        lax.fori_loop(0, (i + 1) * per_q, chunk, 0)

    o_ref[0] = (acc_scr[...] / l_scr[...]).T.astype(o_ref.dtype)


def _fox_attention(qt, k, vt, kn, d_model):
    bsz, n_heads, qk_dim, seq = qt.shape
    n_c, tc = k.shape[2], k.shape[3]
    dh = vt.shape[3]
    tq = Q_TILE
    resident = lambda shape: pl.BlockSpec(shape, lambda b, h, i: (b, h, 0, 0, 0))
    return pl.pallas_call(
        _fox_kernel,
        grid=(bsz, n_heads, seq // tq),
        in_specs=[
            pl.BlockSpec((1, 1, qk_dim, tq), lambda b, h, i: (b, h, 0, i)),
            resident((1, 1, n_c, tc, qk_dim)),
            resident((1, 1, n_c, dh, tc)),
            pl.BlockSpec((1,) + kn.shape[1:], lambda b, h, i: (b, 0, 0, 0)),
        ],
        out_specs=pl.BlockSpec((1, tq, dh), lambda b, h, i: (b, i, h)),
        out_shape=jax.ShapeDtypeStruct((bsz, seq, d_model), BF16),
        scratch_shapes=[
            pltpu.VMEM((qk_dim, tq), BF16),
            pltpu.VMEM((tq, tq), BF16), pltpu.VMEM((tq, tq), BF16),
            pltpu.VMEM((dh, tq), F32), pltpu.VMEM((1, tq), F32),
            pltpu.VMEM((1, tq), F32)],
        compiler_params=pltpu.CompilerParams(
            dimension_semantics=("arbitrary", "arbitrary", "arbitrary"),
            vmem_limit_bytes=V7X_VMEM_LIMIT_BYTES),
        name="fox_attention",
    )(qt, k, vt, kn)


def _post_kernel(yb_ref, part_ref, g1_ref, h_ref, wpb_ref, wo_ref, gpost_ref, gmlp_ref, wup_ref,
                 wdown_ref, gmlp_post_ref, out_ref):
    merged = part_ref[0] + g1_ref[0] * _dot(yb_ref[0], wpb_ref[...])
    h1 = h_ref[0] + _rms(_dot(merged.astype(BF16), wo_ref[...]), gpost_ref[...])
    u = _rms(h1, gmlp_ref[...]).astype(BF16)
    act = jnp.square(jnp.maximum(_dot(u, wup_ref[...]), 0.0))
    out_ref[0] = h1 + _rms(_dot(act.astype(BF16), wdown_ref[...]), gmlp_post_ref[...])


def _post(yb, part, g1, h, p):
    bsz, seq, d = h.shape
    d_ff = p["w_up"].shape[1]
    tm = POST_ROW_TILE
    row_spec = pl.BlockSpec((1, tm, d), lambda b, j: (b, j, 0))
    return pl.pallas_call(
        _post_kernel,
        grid=(bsz, seq // tm),
        in_specs=[row_spec, row_spec, row_spec, row_spec,
                  _const_spec((d, d)), _const_spec((d, d)), _const_spec((1, d)), _const_spec((1, d)),
                  _const_spec((d, d_ff)), _const_spec((d_ff, d)), _const_spec((1, d))],
        out_specs=row_spec,
        out_shape=jax.ShapeDtypeStruct((bsz, seq, d), F32),
        compiler_params=pltpu.CompilerParams(
            dimension_semantics=("arbitrary", "arbitrary"), vmem_limit_bytes=V7X_VMEM_LIMIT_BYTES),
        name="post",
    )(yb, part, g1, h, p["w_pb"], p["w_o"], p["g_post"], p["g_mlp_pre"], p["w_up"], p["w_down"],
      p["g_mlp_post"])


def _layer_params(l, d, g_mix_pre, w_in, conv_w, conv_b, w_r, b_r, w_i, b_i, lru_lambda, b_f,
                  b_gate, w_pa, w_pb, w_pc, w_o, g_mix_post, g_mlp_pre, w_up, w_down, g_mlp_post):
    wi = w_in[l]
    off_q, off_k, off_v, off_mq, off_gate = d, 2 * d, 3 * d, 4 * d, 5 * d
    off_f = off_gate + N_BRANCH * d
    w_f = jnp.repeat(wi[:, off_f:off_f + FOX_HEADS], N_SPLIT, axis=1)
    pad = BIAS_LANES - FOX_HEADS * N_SPLIT
    row = lambda v: v.reshape(1, -1).astype(F32)
    return dict(
        g_pre=row(g_mix_pre[l]),
        w_lru=wi[:, 0:d].astype(BF16),
        w_k=wi[:, off_k:off_k + d].astype(BF16),
        w_qvt=jnp.concatenate([wi[:, off_q:off_q + d].T, wi[:, off_v:off_v + d].T], axis=0).astype(BF16),
        w_mq=wi[:, off_mq:off_mq + d].astype(BF16),
        w_gate=wi[:, off_gate:off_gate + N_BRANCH * d].astype(BF16),
        w_f=jnp.pad(w_f, ((0, 0), (0, pad))).astype(BF16),
        conv_w=conv_w[l].astype(F32), conv_b=row(conv_b[l]),
        w_ri=jnp.concatenate([w_r[l], w_i[l]], axis=-1).astype(BF16),
        b_r=row(b_r[l]), b_i=row(b_i[l]), lam=row(lru_lambda[l]),
        b_f=jnp.pad(jnp.repeat(b_f[l], N_SPLIT), (0, pad)).reshape(1, -1).astype(F32),
        b_gate=b_gate[l].astype(F32),
        w_pa=w_pa[l].astype(BF16), w_pb=w_pb[l].astype(BF16), w_pc=w_pc[l].astype(BF16),
        w_o=w_o[l].astype(BF16), g_post=row(g_mix_post[l]), g_mlp_pre=row(g_mlp_pre[l]),
        w_up=w_up[l].astype(BF16), w_down=w_down[l].astype(BF16), g_mlp_post=row(g_mlp_post[l]),
    )


def kernel(x, mem, g_mix_pre, w_in, conv_w, conv_b, w_r, b_r, w_i, b_i, lru_lambda, b_f, g_mem, w_mem_kv, b_gate, w_pa, w_pb, w_pc, w_o, g_mix_post, g_mlp_pre, w_up, w_down, g_mlp_post):
    bsz, seq, d = x.shape
    depth = w_in.shape[0]
    assert seq % Q_TILE == 0 and Q_TILE % (2 * ROW_TILE) == 0 and seq % POST_ROW_TILE == 0
    assert d % (FOX_HEADS * V7X_LANES) == 0
    kv_all = _mem_kv(mem, g_mem, w_mem_kv)
    h = x
    for l in range(depth):
        p = _layer_params(l, d, g_mix_pre, w_in, conv_w, conv_b, w_r, b_r, w_i, b_i, lru_lambda,
                          b_f, b_gate, w_pa, w_pb, w_pc, w_o, g_mix_post, g_mlp_pre, w_up, w_down,
                          g_mlp_post)
        qt, k, vt, kn, part, g1 = _mixer_in(h, kv_all[l], p)
        yb = _fox_attention(qt, k, vt, kn, d)
        h = _post(yb, part, g1, h, p)
    return h
```

```python
import functools

import jax
import jax.numpy as jnp
from jax import lax
from jax.experimental import pallas as pl
from jax.experimental.pallas import tpu as pltpu

EPS = 1e-6
LRU_BLOCKS = 8
LRU_C = 8.0
CONV_W = 4
FOX_HEADS = 8
MEM_HEADS = 4
N_BRANCH = 3
LOG2E = 1.4426950408889634
NEG_BIG = -1e30

V7X_LANES = 128
V7X_SUBLANES = 8
V7X_BF16_SUBLANES = 16
V7X_VMEM_LIMIT_BYTES = 56 * 1024 * 1024

ROW_TILE = 512
POST_ROW_TILE = 512
Q_TILE = 2048
BIAS_LANES = V7X_LANES
N_SPLIT = 3
REF_LANE0 = 2 * FOX_HEADS * N_SPLIT
REF_SLACK = 1.01
L_UNDERFLOW = 2.0 ** -60
assert REF_LANE0 % V7X_BF16_SUBLANES == 0 and REF_LANE0 + N_SPLIT <= BIAS_LANES

F32 = jnp.float32
BF16 = jnp.bfloat16


def _const_spec(shape):
    nd = len(shape)
    return pl.BlockSpec(shape, lambda *_: (0,) * nd, pipeline_mode=pl.Buffered(1))


def _rms(x, g):
    return x * lax.rsqrt(jnp.mean(x * x, axis=-1, keepdims=True) + EPS) * g


def _sigmoid(x):
    return 1.0 / (1.0 + jnp.exp2(x * (-LOG2E)))


def _softplus(x):
    return jnp.maximum(x, 0.0) + jnp.log1p(jnp.exp(-jnp.abs(x)))


def _dot(a, b):
    return jnp.dot(a, b, preferred_element_type=F32)


def _dot_nt(a, b):
    return lax.dot_general(a, b, (((1,), (1,)), ((), ())), preferred_element_type=F32)


def _mem_kv_kernel(mem_ref, g_ref, w_ref, kv_ref):
    u = _rms(mem_ref[0], g_ref[0]).astype(BF16)
    kv_ref[0, 0] = _dot(u, w_ref[0]).astype(BF16)


def _mem_kv(mem, g_mem, w_mem_kv):
    depth, d_model, n_kv = w_mem_kv.shape
    bsz, n_mem, _ = mem.shape
    return pl.pallas_call(
        _mem_kv_kernel,
        grid=(depth, bsz),
        in_specs=[
            pl.BlockSpec((1, n_mem, d_model), lambda l, b: (b, 0, 0)),
            pl.BlockSpec((1, 1, d_model), lambda l, b: (l, 0, 0)),
            pl.BlockSpec((1, d_model, n_kv), lambda l, b: (l, 0, 0)),
        ],
        out_specs=pl.BlockSpec((1, 1, n_mem, n_kv), lambda l, b: (l, b, 0, 0)),
        out_shape=jax.ShapeDtypeStruct((depth, bsz, n_mem, n_kv), BF16),
        compiler_params=pltpu.CompilerParams(
            dimension_semantics=("arbitrary", "arbitrary"), vmem_limit_bytes=V7X_VMEM_LIMIT_BYTES),
        name="mem_kv",
    )(mem, g_mem.reshape(depth, 1, d_model), w_mem_kv.astype(BF16))


def _mixer_in_kernel(h_ref, gpre_ref, wlru_ref, wk_ref, wqvt_ref, wmq_ref, wgate_ref, wf_ref,
                     convw_ref, convb_ref, wri_ref, br_ref, bi_ref, lam_ref, bf_ref, kv_ref,
                     bgate_ref, wpa_ref, wpc_ref,
                     qt_ref, k_ref, vt_ref, kn_ref, part_ref, g1_ref,
                     conv_car, a_scr, b_scr, hcar, ccar, *, q_scale):
    tm, d = h_ref.shape[1], h_ref.shape[2]
    n_grp = tm // V7X_SUBLANES
    lru_blk = d // LRU_BLOCKS
    fox_dh = d // FOX_HEADS
    mem_dh = d // MEM_HEADS

    @pl.when(pl.program_id(1) == 0)
    def _():
        conv_car[...] = jnp.zeros_like(conv_car)
        hcar[...] = jnp.zeros_like(hcar)
        ccar[...] = jnp.zeros_like(ccar)

    u = _rms(h_ref[0], gpre_ref[...]).astype(BF16)

    z = _dot(u, wlru_ref[...])
    first_row = lax.broadcasted_iota(jnp.int32, z.shape, 0) == 0
    y = convw_ref[0:1, :] * z
    for kk in range(1, CONV_W):
        rot = pltpu.roll(y, 1, 0)
        y = jnp.where(first_row, conv_car[kk - 1:kk, :], rot) + convw_ref[kk:kk + 1, :] * z
        conv_car[kk - 1:kk, :] = rot[0:1, :]
    xc = y + convb_ref[...]
    xcb = xc.astype(BF16)
    neg_c_sp = (-LRU_C * LOG2E) * _softplus(-lam_ref[...])
    for hb in range(LRU_BLOCKS):
        sl = slice(hb * lru_blk, (hb + 1) * lru_blk)
        ri = _dot(xcb[:, sl], wri_ref[hb])
        r = _sigmoid(ri[:, :lru_blk] + br_ref[:, sl])
        i = _sigmoid(ri[:, lru_blk:] + bi_ref[:, sl])
        a = jnp.exp2(r * neg_c_sp[:, sl])
        a_scr[:, sl] = a
        b_scr[:, sl] = jnp.exp2(0.5 * jnp.log2(1.0 - a * a)) * i * xc[:, sl]
    a3 = a_scr[...].reshape(n_grp, V7X_SUBLANES, d)
    b3 = b_scr[...].reshape(n_grp, V7X_SUBLANES, d)
    sub = lax.broadcasted_iota(jnp.int32, a3.shape, 1)
    step = 1
    while step < V7X_SUBLANES:
        keep = sub >= step
        a_prev = jnp.where(keep, pltpu.roll(a3, step, 1), 1.0)
        b_prev = jnp.where(keep, pltpu.roll(b3, step, 1), 0.0)
        b3 = a3 * b_prev + b3
        a3 = a3 * a_prev
        step *= 2
    carry = hcar[...]
    for g in range(n_grp):
        hg = a3[g] * carry + b3[g]
        b_scr[g * V7X_SUBLANES:(g + 1) * V7X_SUBLANES, :] = hg
        carry = hg[V7X_SUBLANES - 1:V7X_SUBLANES, :]
    hcar[...] = carry
    pa = _dot(b_scr[...].astype(BF16), wpa_ref[...])

    kf = _dot(u, wk_ref[...])
    kz = kf.astype(BF16)
    kn_lane = lax.broadcasted_iota(jnp.int32, (1, BIAS_LANES), 1)
    kn = jnp.zeros((1, BIAS_LANES), F32)
    for hh in range(FOX_HEADS):
        kh = kf[:, hh * fox_dh:(hh + 1) * fox_dh]
        n2 = jnp.max(jnp.sum(kh * kh, axis=1, keepdims=True), axis=0, keepdims=True)
        kn = jnp.where(kn_lane == hh, n2, kn)
    kn_ref[0, 0] = kn
    zt = _dot_nt(wqvt_ref[...], u)
    qt = (zt[:d] * q_scale).astype(BF16)
    vt = zt[d:].astype(BF16)
    zf = _dot(u, wf_ref[...]) + bf_ref[...]
    lf = (jnp.minimum(zf, 0.0) - jnp.log1p(jnp.exp(-jnp.abs(zf)))) * LOG2E
    c3 = lf.reshape(n_grp, V7X_SUBLANES, BIAS_LANES)
    sub = lax.broadcasted_iota(jnp.int32, c3.shape, 1)
    step = 1
    while step < V7X_SUBLANES:
        c3 = c3 + jnp.where(sub >= step, pltpu.roll(c3, step, 1), 0.0)
        step *= 2
    ccarry = ccar[...]
    c_rows = []
    for g in range(n_grp):
        cg = c3[g] + ccarry
        c_rows.append(cg)
        ccarry = cg[V7X_SUBLANES - 1:V7X_SUBLANES, :]
    ccar[...] = ccarry
    c = jnp.concatenate(c_rows, axis=0)
    c_hi = c.astype(BF16).astype(F32)
    c_mid = (c - c_hi).astype(BF16).astype(F32)
    c_lo = (c - c_hi - c_mid).astype(BF16).astype(F32)
    lane = lax.broadcasted_iota(jnp.int32, c.shape, 1)
    piece = lane % N_SPLIT
    c_split = jnp.where(piece == 0, c_hi, jnp.where(piece == 1, c_mid, c_lo))
    n_bias = FOX_HEADS * N_SPLIT
    k_bias = jnp.where(lane < n_bias, -c_split,
                       jnp.where(lane < REF_LANE0 + N_SPLIT, 1.0, 0.0)).astype(BF16)
    c_split_t = pltpu.roll(c_split.T, n_bias, 0)
    row = lax.broadcasted_iota(jnp.int32, c_split_t.shape, 0)
    for hh in range(FOX_HEADS):
        sl = slice(hh * fox_dh, (hh + 1) * fox_dh)
        lo = hh * N_SPLIT
        ones_rows = (row >= lo) & (row < lo + N_SPLIT)
        c_rows_sel = (row >= n_bias + lo) & (row < n_bias + lo + N_SPLIT)
        q_bias = jnp.where(ones_rows, 1.0, jnp.where(c_rows_sel, c_split_t, 0.0)).astype(BF16)
        k_ref[0, hh, 0, :, 0:fox_dh] = kz[:, sl]
        k_ref[0, hh, 0, :, fox_dh:fox_dh + BIAS_LANES] = k_bias
        qt_ref[0, hh, 0:fox_dh, :] = qt[sl, :]
        qt_ref[0, hh, fox_dh:fox_dh + BIAS_LANES, :] = q_bias
        vt_ref[0, hh, 0] = vt[sl, :]

    mq = (_dot(u, wmq_ref[...]) * (mem_dh ** -0.5)).astype(BF16)
    yc = []
    for hh in range(MEM_HEADS):
        sl = slice(hh * mem_dh, (hh + 1) * mem_dh)
        lg = _dot_nt(mq[:, sl], kv_ref[0, :, sl])
        e = jnp.exp(lg - jnp.max(lg, axis=-1, keepdims=True))
        num = _dot(e.astype(BF16), kv_ref[0, :, d + hh * mem_dh:d + (hh + 1) * mem_dh])
        yc.append((num / jnp.sum(e, axis=-1, keepdims=True)).astype(BF16))
    pc = _dot(jnp.concatenate(yc, axis=1), wpc_ref[...])

    g0 = _sigmoid(_dot(u, wgate_ref[:, 0:d]) + bgate_ref[0:1, :])
    g2 = _sigmoid(_dot(u, wgate_ref[:, 2 * d:3 * d]) + bgate_ref[2:3, :])
    part_ref[0] = (g0 * pa + g2 * pc).astype(part_ref.dtype)
    g1_ref[0] = _sigmoid(_dot(u, wgate_ref[:, d:2 * d]) + bgate_ref[1:2, :]).astype(g1_ref.dtype)


def _mixer_in(h, kv, p):
    bsz, seq, d = h.shape
    tm = ROW_TILE
    n_t = seq // tm
    fox_dh = d // FOX_HEADS
    n_mem, n_kv = kv.shape[1], kv.shape[2]
    qk_dim = fox_dh + BIAS_LANES
    row_spec = pl.BlockSpec((1, tm, d), lambda b, j: (b, j, 0))
    in_specs = [
        row_spec,
        _const_spec((1, d)),
        _const_spec((d, d)), _const_spec((d, d)), _const_spec((2 * d, d)), _const_spec((d, d)),
        _const_spec((d, N_BRANCH * d)), _const_spec((d, BIAS_LANES)),
        _const_spec((CONV_W, d)), _const_spec((1, d)),
        _const_spec((LRU_BLOCKS, d // LRU_BLOCKS, 2 * d // LRU_BLOCKS)),
        _const_spec((1, d)), _const_spec((1, d)), _const_spec((1, d)), _const_spec((1, BIAS_LANES)),
        pl.BlockSpec((1, n_mem, n_kv), lambda b, j: (b, 0, 0)),
        _const_spec((N_BRANCH, d)), _const_spec((d, d)), _const_spec((d, d)),
    ]
    out_specs = [
        pl.BlockSpec((1, FOX_HEADS, qk_dim, tm), lambda b, j: (b, 0, 0, j)),
        pl.BlockSpec((1, FOX_HEADS, 1, tm, qk_dim), lambda b, j: (b, 0, j, 0, 0)),
        pl.BlockSpec((1, FOX_HEADS, 1, fox_dh, tm), lambda b, j: (b, 0, j, 0, 0)),
        pl.BlockSpec((1, 1, 1, BIAS_LANES), lambda b, j: (b, j, 0, 0)),
        row_spec, row_spec,
    ]
    out_shape = [
        jax.ShapeDtypeStruct((bsz, FOX_HEADS, qk_dim, seq), BF16),
        jax.ShapeDtypeStruct((bsz, FOX_HEADS, n_t, tm, qk_dim), BF16),
        jax.ShapeDtypeStruct((bsz, FOX_HEADS, n_t, fox_dh, tm), BF16),
        jax.ShapeDtypeStruct((bsz, n_t, 1, BIAS_LANES), F32),
        jax.ShapeDtypeStruct((bsz, seq, d), BF16),
        jax.ShapeDtypeStruct((bsz, seq, d), BF16),
    ]
    scratch = [
        pltpu.VMEM((CONV_W - 1, d), F32),
        pltpu.VMEM((tm, d), F32), pltpu.VMEM((tm, d), F32),
        pltpu.VMEM((1, d), F32), pltpu.VMEM((1, BIAS_LANES), F32),
    ]
    return pl.pallas_call(
        functools.partial(_mixer_in_kernel, q_scale=(fox_dh ** -0.5) * LOG2E),
        grid=(bsz, n_t),
        in_specs=in_specs, out_specs=out_specs, out_shape=out_shape, scratch_shapes=scratch,
        compiler_params=pltpu.CompilerParams(
            dimension_semantics=("arbitrary", "arbitrary"), vmem_limit_bytes=V7X_VMEM_LIMIT_BYTES),
        name="mixer_in",
    )(h, p["g_pre"], p["w_lru"], p["w_k"], p["w_qvt"], p["w_mq"], p["w_gate"], p["w_f"],
      p["conv_w"], p["conv_b"], p["w_ri"], p["b_r"], p["b_i"], p["lam"], p["b_f"], kv,
      p["b_gate"], p["w_pa"], p["w_pc"])


def _fox_kernel(qt_ref, k_ref, vt_ref, kn_ref, o_ref, q_eff, p0, p1, acc_scr, l_scr, m_scr):
    tq = qt_ref.shape[3]
    tc = k_ref.shape[3]
    dh = o_ref.shape[2]
    per_q = tq // tc
    i = pl.program_id(2)
    p_buf = (p0, p1)
    ref_row0 = dh + REF_LANE0

    kn = jnp.max(kn_ref[0], axis=0)
    head_lane = lax.broadcasted_iota(jnp.int32, kn.shape, 1) == pl.program_id(1)
    kmax2 = jnp.max(jnp.where(head_lane, kn, 0.0), axis=1, keepdims=True)
    q = qt_ref[0, 0]
    qf = q[0:dh].astype(F32)
    ref = jnp.sqrt(jnp.sum(qf * qf, axis=0, keepdims=True) * kmax2) * REF_SLACK
    r_hi = ref.astype(BF16).astype(F32)
    r_mid = (ref - r_hi).astype(BF16).astype(F32)
    r_lo = (ref - r_hi - r_mid).astype(BF16).astype(F32)
    row = lax.broadcasted_iota(jnp.int32, (V7X_BF16_SUBLANES, tq), 0)
    q_eff[...] = q
    q_eff[ref_row0:ref_row0 + V7X_BF16_SUBLANES, :] = jnp.where(
        row == 0, -r_hi, jnp.where(row == 1, -r_mid, jnp.where(row == 2, -r_lo, 0.0))).astype(BF16)

    def probs_diag():
        l_new = None
        for sub in range(per_q):
            c0 = sub * tc
            s = _dot(k_ref[0, 0, i * per_q + sub], q_eff[:, c0:tq])
            key = lax.broadcasted_iota(jnp.int32, s.shape, 0)
            qry = lax.broadcasted_iota(jnp.int32, s.shape, 1)
            p = jnp.exp2(jnp.where(key <= qry, s, NEG_BIG))
            p0[c0:c0 + tc, c0:tq] = p.astype(BF16)
            l_sub = jnp.sum(p, axis=0, keepdims=True)
            if sub > 0:
                l_sub = jnp.concatenate([jnp.zeros((1, c0), F32), l_sub], axis=1)
            l_new = l_sub if l_new is None else l_new + l_sub
        l_scr[...] = l_new

    def accumulate_diag(subs):
        upd = None
        for sub in subs:
            c0 = sub * tc
            term = _dot(vt_ref[0, 0, i * per_q + sub], p0[c0:c0 + tc, c0:tq])
            if sub > 0:
                term = jnp.concatenate([jnp.zeros((dh, c0), F32), term], axis=1)
            upd = term if upd is None else upd + term
        if subs[0] == 0:
            acc_scr[...] = upd
        else:
            acc_scr[...] += upd

    def probs(tile, slot):
        l_new = l_scr[...]
        for sub in range(per_q):
            p = jnp.exp2(_dot(k_ref[0, 0, tile * per_q + sub], q_eff[...]))
            l_new = l_new + jnp.sum(p, axis=0, keepdims=True)
            p_buf[slot][sub * tc:(sub + 1) * tc, :] = p.astype(BF16)
        l_scr[...] = l_new

    def accumulate(tile, slot):
        upd = None
        for sub in range(per_q):
            term = _dot(vt_ref[0, 0, tile * per_q + sub], p_buf[slot][sub * tc:(sub + 1) * tc, :])
            upd = term if upd is None else upd + term
        acc_scr[...] += upd

    def by_parity(x, fn):
        for par in range(2):
            pl.when(x % 2 == par)(functools.partial(fn, par))

    diag_early, diag_late = range(0, per_q // 2), range(per_q // 2, per_q)
    probs_diag()
    accumulate_diag(diag_early)

    @pl.when(i == 0)
    def _():
        accumulate_diag(diag_late)

    @pl.when(i > 0)
    def _():
        probs(0, 1)
        accumulate_diag(diag_late)

        def step(t, carry):
            def one(par):
                probs(t, 1 - par)
                accumulate(t - 1, par)
            by_parity(t, one)
            return carry

        lax.fori_loop(1, i, step, 0)
        by_parity(i, lambda par: accumulate(i - 1, par))

    @pl.when(jnp.logical_not(jnp.min(l_scr[...]) >= L_UNDERFLOW))
    def _():
        m_scr[...] = jnp.full_like(m_scr, NEG_BIG)
        l_scr[...] = jnp.zeros_like(l_scr)
        acc_scr[...] = jnp.zeros_like(acc_scr)

        def chunk(c, carry):
            s = _dot(k_ref[0, 0, c], qt_ref[0, 0])
            key = lax.broadcasted_iota(jnp.int32, s.shape, 0) + c * tc
            qry = lax.broadcasted_iota(jnp.int32, s.shape, 1) + i * tq
            s = jnp.where(key <= qry, s, NEG_BIG)
            m_old = m_scr[...]
            m_new = jnp.maximum(m_old, jnp.max(s, axis=0, keepdims=True))
            p = jnp.exp2(s - m_new)
            alpha = jnp.exp2(m_old - m_new)
            l_scr[...] = alpha * l_scr[...] + jnp.sum(p, axis=0, keepdims=True)
            acc_scr[...] = alpha * acc_scr[...] + _dot(vt_ref[0, 0, c], p.astype(BF16))
            m_scr[...] = m_new
            return carry

        lax.fori_loop(0, (i + 1) * per_q, chunk, 0)

    o_ref[0] = (acc_scr[...] / l_scr[...]).T.astype(o_ref.dtype)


def _fox_attention(qt, k, vt, kn, d_model):
    bsz, n_heads, qk_dim, seq = qt.shape
    n_c, tc = k.shape[2], k.shape[3]
    dh = vt.shape[3]
    tq = Q_TILE
    resident = lambda shape: pl.BlockSpec(shape, lambda b, h, i: (b, h, 0, 0, 0))
    return pl.pallas_call(
        _fox_kernel,
        grid=(bsz, n_heads, seq // tq),
        in_specs=[
            pl.BlockSpec((1, 1, qk_dim, tq), lambda b, h, i: (b, h, 0, i)),
            resident((1, 1, n_c, tc, qk_dim)),
            resident((1, 1, n_c, dh, tc)),
            pl.BlockSpec((1,) + kn.shape[1:], lambda b, h, i: (b, 0, 0, 0)),
        ],
        out_specs=pl.BlockSpec((1, tq, dh), lambda b, h, i: (b, i, h)),
        out_shape=jax.ShapeDtypeStruct((bsz, seq, d_model), BF16),
        scratch_shapes=[
            pltpu.VMEM((qk_dim, tq), BF16),
            pltpu.VMEM((tq, tq), BF16), pltpu.VMEM((tq, tq), BF16),
            pltpu.VMEM((dh, tq), F32), pltpu.VMEM((1, tq), F32),
            pltpu.VMEM((1, tq), F32)],
        compiler_params=pltpu.CompilerParams(
            dimension_semantics=("arbitrary", "arbitrary", "arbitrary"),
            vmem_limit_bytes=V7X_VMEM_LIMIT_BYTES),
        name="fox_attention",
    )(qt, k, vt, kn)


def _post_kernel(yb_ref, part_ref, g1_ref, h_ref, wpb_ref, wo_ref, gpost_ref, gmlp_ref, wup_ref,
                 wdown_ref, gmlp_post_ref, out_ref):
    merged = part_ref[0].astype(F32) + g1_ref[0].astype(F32) * _dot(yb_ref[0], wpb_ref[...])
    h1 = h_ref[0] + _rms(_dot(merged.astype(BF16), wo_ref[...]), gpost_ref[...])
    u = _rms(h1, gmlp_ref[...]).astype(BF16)
    act = jnp.square(jnp.maximum(_dot(u, wup_ref[...]), 0.0))
    out_ref[0] = h1 + _rms(_dot(act.astype(BF16), wdown_ref[...]), gmlp_post_ref[...])


def _post(yb, part, g1, h, p):
    bsz, seq, d = h.shape
    d_ff = p["w_up"].shape[1]
    tm = POST_ROW_TILE
    row_spec = pl.BlockSpec((1, tm, d), lambda b, j: (b, j, 0))
    return pl.pallas_call(
        _post_kernel,
        grid=(bsz, seq // tm),
        in_specs=[row_spec, row_spec, row_spec, row_spec,
                  _const_spec((d, d)), _const_spec((d, d)), _const_spec((1, d)), _const_spec((1, d)),
                  _const_spec((d, d_ff)), _const_spec((d_ff, d)), _const_spec((1, d))],
        out_specs=row_spec,
        out_shape=jax.ShapeDtypeStruct((bsz, seq, d), F32),
        compiler_params=pltpu.CompilerParams(
            dimension_semantics=("arbitrary", "arbitrary"), vmem_limit_bytes=V7X_VMEM_LIMIT_BYTES),
        name="post",
    )(yb, part, g1, h, p["w_pb"], p["w_o"], p["g_post"], p["g_mlp_pre"], p["w_up"], p["w_down"],
      p["g_mlp_post"])


def _layer_params(l, d, g_mix_pre, w_in, conv_w, conv_b, w_r, b_r, w_i, b_i, lru_lambda, b_f,
                  b_gate, w_pa, w_pb, w_pc, w_o, g_mix_post, g_mlp_pre, w_up, w_down, g_mlp_post):
    wi = w_in[l]
    off_q, off_k, off_v, off_mq, off_gate = d, 2 * d, 3 * d, 4 * d, 5 * d
    off_f = off_gate + N_BRANCH * d
    w_f = jnp.repeat(wi[:, off_f:off_f + FOX_HEADS], N_SPLIT, axis=1)
    pad = BIAS_LANES - FOX_HEADS * N_SPLIT
    row = lambda v: v.reshape(1, -1).astype(F32)
    return dict(
        g_pre=row(g_mix_pre[l]),
        w_lru=wi[:, 0:d].astype(BF16),
        w_k=wi[:, off_k:off_k + d].astype(BF16),
        w_qvt=jnp.concatenate([wi[:, off_q:off_q + d].T, wi[:, off_v:off_v + d].T], axis=0).astype(BF16),
        w_mq=wi[:, off_mq:off_mq + d].astype(BF16),
        w_gate=wi[:, off_gate:off_gate + N_BRANCH * d].astype(BF16),
        w_f=jnp.pad(w_f, ((0, 0), (0, pad))).astype(BF16),
        conv_w=conv_w[l].astype(F32), conv_b=row(conv_b[l]),
        w_ri=jnp.concatenate([w_r[l], w_i[l]], axis=-1).astype(BF16),
        b_r=row(b_r[l]), b_i=row(b_i[l]), lam=row(lru_lambda[l]),
        b_f=jnp.pad(jnp.repeat(b_f[l], N_SPLIT), (0, pad)).reshape(1, -1).astype(F32),
        b_gate=b_gate[l].astype(F32),
        w_pa=w_pa[l].astype(BF16), w_pb=w_pb[l].astype(BF16), w_pc=w_pc[l].astype(BF16),
        w_o=w_o[l].astype(BF16), g_post=row(g_mix_post[l]), g_mlp_pre=row(g_mlp_pre[l]),
        w_up=w_up[l].astype(BF16), w_down=w_down[l].astype(BF16), g_mlp_post=row(g_mlp_post[l]),
    )


def kernel(x, mem, g_mix_pre, w_in, conv_w, conv_b, w_r, b_r, w_i, b_i, lru_lambda, b_f, g_mem, w_mem_kv, b_gate, w_pa, w_pb, w_pc, w_o, g_mix_post, g_mlp_pre, w_up, w_down, g_mlp_post):
    bsz, seq, d = x.shape
    depth = w_in.shape[0]
    assert seq % Q_TILE == 0 and Q_TILE % (2 * ROW_TILE) == 0 and seq % POST_ROW_TILE == 0
    assert d % (FOX_HEADS * V7X_LANES) == 0
    kv_all = _mem_kv(mem, g_mem, w_mem_kv)
    h = x
    for l in range(depth):
        p = _layer_params(l, d, g_mix_pre, w_in, conv_w, conv_b, w_r, b_r, w_i, b_i, lru_lambda,
                          b_f, b_gate, w_pa, w_pb, w_pc, w_o, g_mix_post, g_mlp_pre, w_up, w_down,
                          g_mlp_post)
        qt, k, vt, kn, part, g1 = _mixer_in(h, kv_all[l], p)
        yb = _fox_attention(qt, k, vt, kn, d)
        h = _post(yb, part, g1, h, p)
    return h
```
